```python
import math
import jax, jax.numpy as jnp
from jax import lax
import numpy as np

D_MODEL = 1024
BATCH = 4
SEQ = 4096
DEPTH = 2
DEC_BATCH = 32
DEC_SEQ = 8
PAST_LEN = 8192
PAGE_SIZE = 128

N_AB_LAYERS = (DEPTH + 1) // 2
N_LRU_LAYERS = DEPTH // 2
H_FOX = 8
DH_FOX = 64
H_DIFF = 4
DH_DIFF = 64
D_FOX = H_FOX * DH_FOX
D_DIFF = H_DIFF * 2 * DH_DIFF
D_IN_AB = 3 * D_FOX + H_FOX + 3 * D_DIFF
D_MIX_AB = D_FOX + D_DIFF
ROT_DIMS = DH_DIFF // 4
ROPE_THETA = 500000.0
Q_BLOCK = 128
D_RNN = 1280
N_LRU_BLOCKS = 16
LRU_BW = D_RNN // N_LRU_BLOCKS
LRU_CONV_W = 4
LRU_C = 8.0
D_FF = 3072
FFN_CONV_W = 3
EPS = 1e-6
NEG = -1e30

kernel_name = "fox_diffattn_rglru_convffn_step"


def _rmsnorm(x, g):
    xf = x.astype(jnp.float32)
    y = xf * lax.rsqrt(jnp.mean(xf * xf, axis=-1, keepdims=True) + EPS)
    return (y * g.astype(jnp.float32)).astype(x.dtype)


def _rope_partial(x, pos):
    half = ROT_DIMS // 2
    inv = ROPE_THETA ** (-jnp.arange(half, dtype=jnp.float32) / half)
    ang = pos.astype(jnp.float32)[:, None] * inv[None, :]
    cos = jnp.cos(ang)[None, :, None, None, :]
    sin = jnp.sin(ang)[None, :, None, None, :]
    xr = x[..., :ROT_DIMS].astype(jnp.float32)
    x1, x2 = xr[..., :half], xr[..., half:]
    rot = jnp.concatenate([x1 * cos - x2 * sin, x2 * cos + x1 * sin], axis=-1).astype(x.dtype)
    return jnp.concatenate([rot, x[..., ROT_DIMS:]], axis=-1)


def _causal_dwconv(u, buf, w, b):
    W = w.shape[0]
    T = u.shape[1]
    up = jnp.concatenate([buf.astype(u.dtype), u], axis=1)
    y = b
    for k in range(W):
        y = y + up[:, k:k + T] * w[k]
    return y, up[:, -(W - 1):]


def _ab_attend(qf, cq, qd, qpos, segs, lam, lam_init, subln_g):
    sf, sd, lens = [], [], []
    for kf, vf, ck, kd, vd, kpos in segs:
        mask = kpos[None, :] <= qpos[:, None]
        s = jnp.einsum('bqhd,bkhd->bhqk', qf, kf).astype(jnp.float32) * (DH_FOX ** -0.5)
        s = s + jnp.transpose(cq, (0, 2, 1))[:, :, :, None] - jnp.transpose(ck, (0, 2, 1))[:, :, None, :]
        sf.append(jnp.where(mask, s, NEG))
        s2 = jnp.einsum('bqhnd,bkhnd->bhnqk', qd, kd).astype(jnp.float32) * (DH_DIFF ** -0.5)
        sd.append(jnp.where(mask, s2, NEG))
        lens.append(kf.shape[1])
    pf = jax.nn.softmax(jnp.concatenate(sf, axis=-1), axis=-1)
    pd = jax.nn.softmax(jnp.concatenate(sd, axis=-1), axis=-1)
    pdiff = pd[:, :, 0] - lam * pd[:, :, 1]
    splits = [sum(lens[:i + 1]) for i in range(len(lens) - 1)]
    of = 0.0
    od = 0.0
    for pfs, pds, seg in zip(jnp.split(pf, splits, axis=-1), jnp.split(pdiff, splits, axis=-1), segs):
        vf, vd = seg[1], seg[4]
        of = of + jnp.einsum('bhqk,bkhd->bqhd', pfs.astype(vf.dtype), vf)
        od = od + jnp.einsum('bhqk,bkhe->bqhe', pds.astype(vd.dtype), vd)
    od = _rmsnorm(od, subln_g) * (1.0 - lam_init)
    b, q = qf.shape[:2]
    return jnp.concatenate([of.reshape(b, q, D_FOX), od.reshape(b, q, D_DIFF)], axis=-1)


def _ab_mixer(h, pos, past, j, layer_idx, W):
    b, t, _ = h.shape
    proj = h @ W['ab_w_in'][j]
    idx = [D_FOX, 2 * D_FOX, 3 * D_FOX, 3 * D_FOX + H_FOX,
           3 * D_FOX + H_FOX + D_DIFF, 3 * D_FOX + H_FOX + 2 * D_DIFF]
    qf, kf, vf, fl, qd, kd, vd = jnp.split(proj, idx, axis=-1)
    qf = qf.reshape(b, t, H_FOX, DH_FOX)
    kf = kf.reshape(b, t, H_FOX, DH_FOX)
    vf = vf.reshape(b, t, H_FOX, DH_FOX)
    logf = jax.nn.log_sigmoid((fl + W['ab_b_f'][j]).astype(jnp.float32))
    qd = _rope_partial(qd.reshape(b, t, H_DIFF, 2, DH_DIFF), pos)
    kd = _rope_partial(kd.reshape(b, t, H_DIFF, 2, DH_DIFF), pos)
    vd = vd.reshape(b, t, H_DIFF, 2 * DH_DIFF)
    lam_init = 0.8 - 0.6 * math.exp(-0.3 * layer_idx)
    lq1, lk1 = W['ab_lam_q1'][j].astype(jnp.float32), W['ab_lam_k1'][j].astype(jnp.float32)
    lq2, lk2 = W['ab_lam_q2'][j].astype(jnp.float32), W['ab_lam_k2'][j].astype(jnp.float32)
    lam = jnp.exp(jnp.sum(lq1 * lk1)) - jnp.exp(jnp.sum(lq2 * lk2)) + lam_init
    g = W['ab_subln_g'][j]
    if past is None:
        c = jnp.cumsum(logf, axis=1)
        segs = [(kf, vf, c, kd, vd, pos)]
        if t % Q_BLOCK == 0 and t > Q_BLOCK:
            nb = t // Q_BLOCK
            def to_blocks(a):
                return jnp.moveaxis(a.reshape((b, nb, Q_BLOCK) + a.shape[2:]), 1, 0)
            def blk(xs):
                qf_b, cq_b, qd_b, qpos_b = xs
                return _ab_attend(qf_b, cq_b, qd_b, qpos_b, segs, lam, lam_init, g)
            out = lax.map(blk, (to_blocks(qf), to_blocks(c), to_blocks(qd), pos.reshape(nb, Q_BLOCK)))
            out = jnp.moveaxis(out, 0, 1).reshape(b, t, D_MIX_AB)
        else:
            out = _ab_attend(qf, c, qd, pos, segs, lam, lam_init, g)
    else:
        kf_p, vf_p, logf_p, kd_p, vd_p = past
        P = kf_p.shape[1]
        c_all = jnp.cumsum(jnp.concatenate([logf_p.astype(jnp.float32), logf], axis=1), axis=1)
        c_p, c_new = c_all[:, :P], c_all[:, P:]
        segs = [(kf_p, vf_p, c_p, kd_p, vd_p, jnp.arange(P)),
                (kf, vf, c_new, kd, vd, pos)]
        out = _ab_attend(qf, c_new, qd, pos, segs, lam, lam_init, g)
    y = out @ W['ab_w_out'][j]
    rows = (kf, vf, logf.astype(h.dtype), kd.reshape(b, t, H_DIFF, 2 * DH_DIFF), vd)
    return y, rows


def _lru_mixer(h, conv_buf, h_prev, j, W):
    b, t, _ = h.shape
    gate = jax.nn.gelu(h @ W['lru_w_gate'][j])
    u = h @ W['lru_w_x'][j]
    xc, new_buf = _causal_dwconv(u, conv_buf, W['lru_conv_w'][j], W['lru_conv_b'][j])
    xb = xc.reshape(b, t, N_LRU_BLOCKS, LRU_BW)
    r = jax.nn.sigmoid(jnp.einsum('btnc,ncd->btnd', xb, W['lru_w_a'][j]).reshape(b, t, D_RNN) + W['lru_b_a'][j])
    i = jax.nn.sigmoid(jnp.einsum('btnc,ncd->btnd', xb, W['lru_w_i'][j]).reshape(b, t, D_RNN) + W['lru_b_i'][j])
    log_a = LRU_C * r.astype(jnp.float32) * jax.nn.log_sigmoid(W['lru_lambda'][j].astype(jnp.float32))
    a = jnp.exp(log_a)
    mult = jnp.sqrt(-jnp.expm1(2.0 * log_a))
    bx = mult * (i * xc).astype(jnp.float32)
    bx = bx.at[:, 0].add(a[:, 0] * h_prev.astype(jnp.float32))

    def comb(l, rr):
        a1, b1 = l
        a2, b2 = rr
        return a1 * a2, a2 * b1 + b2

    _, hs = lax.associative_scan(comb, (a, bx), axis=1)
    y = (hs.astype(h.dtype) * gate) @ W['lru_w_out'][j]
    return y, new_buf, hs[:, -1].astype(h.dtype)


def _conv_ffn(h, buf, l, W):
    a = h @ W['ffn_w_a'][l]
    g = h @ W['ffn_w_b'][l]
    ac, new_buf = _causal_dwconv(a, buf, W['ffn_conv_w'][l], W['ffn_conv_b'][l])
    return (jax.nn.gelu(ac) * g) @ W['ffn_w_down'][l], new_buf


def _trunk(x, pos, gather_past, lru_conv0, lru_h0, ffn_buf0, W):
    fk, fv, fl, dk, dv, lc, lh, fc = [], [], [], [], [], [], [], []
    for l in range(DEPTH):
        j = l // 2
        h = _rmsnorm(x, W['mix_norm_g'][l])
        if l % 2 == 0:
            past = None if gather_past is None else gather_past(j)
            y, rows = _ab_mixer(h, pos, past, j, l, W)
            fk.append(rows[0]); fv.append(rows[1]); fl.append(rows[2]); dk.append(rows[3]); dv.append(rows[4])
        else:
            y, nbuf, nh = _lru_mixer(h, lru_conv0[j], lru_h0[j], j, W)
            lc.append(nbuf); lh.append(nh)
        x = x + y
        h = _rmsnorm(x, W['ffn_norm_g'][l])
        y, nbuf = _conv_ffn(h, ffn_buf0[l], l, W)
        fc.append(nbuf)
        x = x + y
    x = _rmsnorm(x, W['final_norm_g'])
    return x, (jnp.stack(fk), jnp.stack(fv), jnp.stack(fl), jnp.stack(dk), jnp.stack(dv),
               jnp.stack(lc), jnp.stack(lh), jnp.stack(fc))


def setup_inputs(seed: int = 0) -> dict:
    key = jax.random.key(seed)
    keys = list(jax.random.split(key, 48))
    cnt = [0]

    def nk():
        k = keys[cnt[0]]
        cnt[0] += 1
        return k

    def nrm(shape, scale):
        return jax.random.normal(nk(), shape, jnp.float32) * scale

    n_pages = PAST_LEN // PAGE_SIZE
    n_used = DEC_BATCH * n_pages
    n_pool = n_used + n_used // 4
    page_table = jax.random.permutation(nk(), n_pool)[:n_used].reshape(DEC_BATCH, n_pages).astype(jnp.int32)

    a0 = jax.random.uniform(nk(), (N_LRU_LAYERS, D_RNN), jnp.float32, 0.9, 0.999)
    s = a0 ** (1.0 / LRU_C)
    lru_lambda = jnp.log(s) - jnp.log1p(-s)

    return {
        "x_prompt": nrm((BATCH, SEQ, D_MODEL), 1.0),
        "x_sample": nrm((DEC_BATCH, DEC_SEQ, D_MODEL), 1.0),
        "cache_fox_k": nrm((N_AB_LAYERS, n_pool, PAGE_SIZE, H_FOX, DH_FOX), 1.0),
        "cache_fox_v": nrm((N_AB_LAYERS, n_pool, PAGE_SIZE, H_FOX, DH_FOX), 1.0),
        "cache_fox_logf": jax.nn.log_sigmoid(3.0 + nrm((N_AB_LAYERS, n_pool, PAGE_SIZE, H_FOX), 0.5)),
        "cache_diff_k": nrm((N_AB_LAYERS, n_pool, PAGE_SIZE, H_DIFF, 2 * DH_DIFF), 1.0),
        "cache_diff_v": nrm((N_AB_LAYERS, n_pool, PAGE_SIZE, H_DIFF, 2 * DH_DIFF), 1.0),
        "state_lru_conv": nrm((N_LRU_LAYERS, DEC_BATCH, LRU_CONV_W - 1, D_RNN), 1.0),
        "state_lru_h": nrm((N_LRU_LAYERS, DEC_BATCH, D_RNN), 0.5),
        "state_ffn_conv": nrm((DEPTH, DEC_BATCH, FFN_CONV_W - 1, D_FF), 1.0),
        "page_table": page_table,
        "mix_norm_g": 1.0 + nrm((DEPTH, D_MODEL), 0.05),
        "ab_w_in": nrm((N_AB_LAYERS, D_MODEL, D_IN_AB), D_MODEL ** -0.5),
        "ab_b_f": 3.0 + nrm((N_AB_LAYERS, H_FOX), 0.5),
        "ab_lam_q1": nrm((N_AB_LAYERS, DH_DIFF), 0.1),
        "ab_lam_k1": nrm((N_AB_LAYERS, DH_DIFF), 0.1),
        "ab_lam_q2": nrm((N_AB_LAYERS, DH_DIFF), 0.1),
        "ab_lam_k2": nrm((N_AB_LAYERS, DH_DIFF), 0.1),
        "ab_subln_g": 1.0 + nrm((N_AB_LAYERS, 2 * DH_DIFF), 0.05),
        "ab_w_out": nrm((N_AB_LAYERS, D_MIX_AB, D_MODEL), D_MIX_AB ** -0.5),
        "lru_w_gate": nrm((N_LRU_LAYERS, D_MODEL, D_RNN), D_MODEL ** -0.5),
        "lru_w_x": nrm((N_LRU_LAYERS, D_MODEL, D_RNN), D_MODEL ** -0.5),
        "lru_conv_w": nrm((N_LRU_LAYERS, LRU_CONV_W, D_RNN), LRU_CONV_W ** -0.5),
        "lru_conv_b": nrm((N_LRU_LAYERS, D_RNN), 0.01),
        "lru_w_a": nrm((N_LRU_LAYERS, N_LRU_BLOCKS, LRU_BW, LRU_BW), LRU_BW ** -0.5),
        "lru_b_a": nrm((N_LRU_LAYERS, D_RNN), 0.01),
        "lru_w_i": nrm((N_LRU_LAYERS, N_LRU_BLOCKS, LRU_BW, LRU_BW), LRU_BW ** -0.5),
        "lru_b_i": nrm((N_LRU_LAYERS, D_RNN), 0.01),
        "lru_lambda": lru_lambda,
        "lru_w_out": nrm((N_LRU_LAYERS, D_RNN, D_MODEL), D_RNN ** -0.5),
        "ffn_norm_g": 1.0 + nrm((DEPTH, D_MODEL), 0.05),
        "ffn_w_a": nrm((DEPTH, D_MODEL, D_FF), D_MODEL ** -0.5),
        "ffn_w_b": nrm((DEPTH, D_MODEL, D_FF), D_MODEL ** -0.5),
        "ffn_conv_w": nrm((DEPTH, FFN_CONV_W, D_FF), FFN_CONV_W ** -0.5),
        "ffn_conv_b": nrm((DEPTH, D_FF), 0.01),
        "ffn_w_down": nrm((DEPTH, D_FF, D_MODEL), D_FF ** -0.5),
        "final_norm_g": 1.0 + nrm((D_MODEL,), 0.05),
    }


def reference(x_prompt, x_sample, cache_fox_k, cache_fox_v, cache_fox_logf, cache_diff_k, cache_diff_v,
              state_lru_conv, state_lru_h, state_ffn_conv, page_table,
              mix_norm_g, ab_w_in, ab_b_f, ab_lam_q1, ab_lam_k1, ab_lam_q2, ab_lam_k2, ab_subln_g, ab_w_out,
              lru_w_gate, lru_w_x, lru_conv_w, lru_conv_b, lru_w_a, lru_b_a, lru_w_i, lru_b_i, lru_lambda, lru_w_out,
              ffn_norm_g, ffn_w_a, ffn_w_b, ffn_conv_w, ffn_conv_b, ffn_w_down, final_norm_g):
    W = dict(mix_norm_g=mix_norm_g, ab_w_in=ab_w_in, ab_b_f=ab_b_f, ab_lam_q1=ab_lam_q1, ab_lam_k1=ab_lam_k1,
             ab_lam_q2=ab_lam_q2, ab_lam_k2=ab_lam_k2, ab_subln_g=ab_subln_g, ab_w_out=ab_w_out,
             lru_w_gate=lru_w_gate, lru_w_x=lru_w_x, lru_conv_w=lru_conv_w, lru_conv_b=lru_conv_b,
             lru_w_a=lru_w_a, lru_b_a=lru_b_a, lru_w_i=lru_w_i, lru_b_i=lru_b_i, lru_lambda=lru_lambda,
             lru_w_out=lru_w_out, ffn_norm_g=ffn_norm_g, ffn_w_a=ffn_w_a, ffn_w_b=ffn_w_b,
             ffn_conv_w=ffn_conv_w, ffn_conv_b=ffn_conv_b, ffn_w_down=ffn_w_down, final_norm_g=final_norm_g)

    bp, tp, _ = x_prompt.shape
    pos_p = jnp.arange(tp)
    z_lc = jnp.zeros((N_LRU_LAYERS, bp, LRU_CONV_W - 1, D_RNN), x_prompt.dtype)
    z_lh = jnp.zeros((N_LRU_LAYERS, bp, D_RNN), x_prompt.dtype)
    z_fc = jnp.zeros((DEPTH, bp, FFN_CONV_W - 1, D_FF), x_prompt.dtype)
    y_prompt, st_p = _trunk(x_prompt, pos_p, None, z_lc, z_lh, z_fc, W)
    p_fox_k, p_fox_v, p_fox_logf, p_diff_k, p_diff_v, p_lru_conv, p_lru_h, p_ffn_conv = st_p

    bs, ts, _ = x_sample.shape
    P = page_table.shape[1] * PAGE_SIZE
    pos_s = P + jnp.arange(ts)

    def gather_past(j):
        def g(c):
            return c[j][page_table].reshape((bs, P) + c.shape[3:])
        return (g(cache_fox_k), g(cache_fox_v), g(cache_fox_logf),
                g(cache_diff_k).reshape(bs, P, H_DIFF, 2, DH_DIFF), g(cache_diff_v))

    y_sample, st_s = _trunk(x_sample, pos_s, gather_past, state_lru_conv, state_lru_h, state_ffn_conv, W)
    s_fox_k, s_fox_v, s_fox_logf, s_diff_k, s_diff_v, s_lru_conv, s_lru_h, s_ffn_conv = st_s

    return (y_prompt, y_sample,
            p_fox_k, p_fox_v, p_fox_logf, p_diff_k, p_diff_v, p_lru_conv, p_lru_h, p_ffn_conv,
            s_fox_k, s_fox_v, s_fox_logf, s_diff_k, s_diff_v, s_lru_conv, s_lru_h, s_ffn_conv)
```

```python
import functools
import math

import jax
import jax.numpy as jnp
from jax import lax
from jax.experimental import pallas as pl
from jax.experimental.pallas import tpu as pltpu

F32 = jnp.float32
BF16 = jnp.bfloat16

D_MODEL = 1024
H_FOX = 8
DH_FOX = 64
H_DIFF = 4
DH_DIFF = 64
D_FOX = H_FOX * DH_FOX
D_DIFF = H_DIFF * 2 * DH_DIFF
D_MIX_AB = D_FOX + D_DIFF
ROT_DIMS = DH_DIFF // 4
ROPE_THETA = 500000.0
D_RNN = 1280
N_LRU_BLOCKS = 16
LRU_BW = D_RNN // N_LRU_BLOCKS
LRU_CONV_W = 4
LRU_C = 8.0
D_FF = 3072
FFN_CONV_W = 3
EPS = 1e-6
NEG = -1e30
PAGE_SIZE = 128

LANES = 128
SUBLANES = 8
VMEM_LIMIT = 56 * 1024 * 1024
N_QKV = 6 * D_FOX
N_INPROJ = N_QKV + LANES


def _params(n_axes):
    return pltpu.CompilerParams(dimension_semantics=("arbitrary",) * n_axes,
                                vmem_limit_bytes=VMEM_LIMIT)


def _full(shape):
    n = len(shape)
    return pl.BlockSpec(shape, lambda *_: (0,) * n)


def _rms(x, g):
    return x * lax.rsqrt(jnp.mean(x * x, axis=-1, keepdims=True) + EPS) * g


def _gelu(x):
    c = math.sqrt(2.0 / math.pi)
    return x * (0.5 * (1.0 + jnp.tanh(c * (x + 0.044715 * (x * x * x)))))


def _log_sigmoid(z):
    return -(jnp.maximum(-z, 0.0) + jnp.log1p(jnp.exp(-jnp.abs(z))))


def _sigmoid(z):
    return 1.0 / (1.0 + jnp.exp(-z))


def _dot(a, b):
    return jnp.dot(a, b, preferred_element_type=F32)


def _dot_nt(a, b):
    return lax.dot_general(a, b, (((1,), (1,)), ((), ())), preferred_element_type=F32)


def _prefix_lanes(x):
    r = lax.broadcasted_iota(jnp.int32, (LANES, LANES), 0)
    c = lax.broadcasted_iota(jnp.int32, (LANES, LANES), 1)
    tri = jnp.where(r <= c, 1.0, 0.0).astype(BF16)
    hi = x.astype(BF16)
    r1 = x - hi.astype(F32)
    mid = r1.astype(BF16)
    lo = (r1 - mid.astype(F32)).astype(BF16)
    return _dot(hi, tri) + _dot(mid, tri) + _dot(lo, tri)


def _inproj_kernel(x_ref, g_ref, w_ref, bf_ref, cos_ref, sin_ref,
                   qf_ref, kf_ref, vf_ref, kfb_ref, vfb_ref, logf_ref, logft_ref,
                   qd_ref, kd_ref, vd_ref, kdb_ref, vdb_ref):
    h = _rms(x_ref[...], g_ref[...]).astype(BF16)

    def proj(i, width=D_FOX):
        return _dot(h, w_ref[:, i * D_FOX:i * D_FOX + width])

    scale = DH_FOX ** -0.5
    qf_ref[...] = (proj(0) * scale).astype(BF16)
    kf = proj(1)
    kf_ref[...] = kf
    kfb_ref[...] = kf.astype(BF16)
    vf = proj(2)
    vf_ref[...] = vf
    vfb_ref[...] = vf.astype(BF16)

    logf = _log_sigmoid(proj(6, LANES) + bf_ref[...])
    logf_ref[...] = logf[:, :H_FOX]
    logft_ref[...] = logf.T[:H_FOX, :]

    cosf = jnp.concatenate([cos_ref[...]] * (D_DIFF // LANES), axis=1)
    sinf = jnp.concatenate([sin_ref[...]] * (D_DIFF // LANES), axis=1)
    lane = lax.broadcasted_iota(jnp.int32, cosf.shape, 1) & (DH_DIFF - 1)
    half = ROT_DIMS // 2

    def rope(x):
        partner = jnp.where(lane < half, pltpu.roll(x, D_DIFF - half, 1), pltpu.roll(x, half, 1))
        return jnp.where(lane < ROT_DIMS, x * cosf + partner * sinf, x)

    qd_ref[...] = (rope(proj(3)) * (DH_DIFF ** -0.5)).astype(BF16)
    kd = rope(proj(4))
    kd_ref[...] = kd
    kdb_ref[...] = kd.astype(BF16)
    vd = proj(5)
    vd_ref[...] = vd
    vdb_ref[...] = vd.astype(BF16)


def _inproj(x2d, g, w_all, bf_pad, cos_t, sin_t, tm):
    m = x2d.shape[0]
    nt = cos_t.shape[0] // tm
    row = lambda n: pl.BlockSpec((tm, n), lambda i: (i, 0))
    tab = pl.BlockSpec((tm, LANES), lambda i: (i % nt, 0))
    f32o = jax.ShapeDtypeStruct((m, D_FOX), F32)
    b16o = jax.ShapeDtypeStruct((m, D_FOX), BF16)
    out_shape = (b16o, f32o, f32o, b16o, b16o,
                 jax.ShapeDtypeStruct((m, H_FOX), F32), jax.ShapeDtypeStruct((H_FOX, m), F32),
                 b16o, f32o, f32o, b16o, b16o)
    out_specs = (row(D_FOX),) * 5 + (row(H_FOX), pl.BlockSpec((H_FOX, tm), lambda i: (0, i))) + (row(D_FOX),) * 5
    return pl.pallas_call(
        _inproj_kernel,
        grid=(m // tm,),
        in_specs=[row(D_MODEL), _full((1, D_MODEL)), _full((D_MODEL, N_INPROJ)), _full((1, LANES)), tab, tab],
        out_specs=out_specs,
        out_shape=out_shape,
        compiler_params=_params(1),
        name="inproj",
    )(x2d, g, w_all, bf_pad, cos_t, sin_t)


def _cumsum_prompt_kernel(x_ref, o_ref, *, t):
    carry = jnp.zeros((H_FOX, 1), F32)
    for i in range(t // LANES):
        c = _prefix_lanes(x_ref[:, i * LANES:(i + 1) * LANES]) + carry
        for j in range(H_FOX // 2):
            o_ref[0, j, :, i * LANES:(i + 1) * LANES] = c[2 * j:2 * j + 2, :]
        carry = c[:, LANES - 1:LANES]


def _cumsum_prompt(logft, b, t):
    return pl.pallas_call(
        functools.partial(_cumsum_prompt_kernel, t=t),
        grid=(b,),
        in_specs=[pl.BlockSpec((H_FOX, t), lambda i: (0, i))],
        out_specs=pl.BlockSpec((1, H_FOX // 2, 2, t), lambda i: (i, 0, 0, 0)),
        out_shape=jax.ShapeDtypeStruct((b, H_FOX // 2, 2, t), F32),
        compiler_params=_params(1),
        name="cumsum_prompt",
    )(logft)


def _cumsum_rows_kernel(x_ref, o_ref):
    o_ref[...] = _prefix_lanes(x_ref[...])


def _cumsum_rows(x, max_rows=2048):
    rows = x.shape[0]
    tr = max(d for d in range(SUBLANES, max_rows + 1, SUBLANES) if rows % d == 0)
    spec = pl.BlockSpec((tr, LANES), lambda i: (i, 0))
    return pl.pallas_call(
        _cumsum_rows_kernel,
        grid=(rows // tr,),
        in_specs=[spec],
        out_specs=spec,
        out_shape=jax.ShapeDtypeStruct((rows, LANES), F32),
        compiler_params=_params(1),
        name="cumsum_pages",
    )(x)


def _stack_masked(q):
    lane = lax.broadcasted_iota(jnp.int32, q.shape, 1)
    zero = jnp.zeros_like(q)
    return jnp.concatenate([jnp.where(lane < LANES // 2, q, zero),
                            jnp.where(lane >= LANES // 2, q, zero)], axis=0)


def _flash_update(u, v, m, l, acc):
    m_new = jnp.maximum(m, jnp.max(u, axis=1, keepdims=True))
    alpha = jnp.exp(m - m_new)
    p = jnp.exp(u - m_new)
    l_new = alpha * l + jnp.sum(p, axis=1, keepdims=True)
    acc_new = alpha * acc + _dot(p.astype(BF16), v)
    return m_new, l_new, acc_new


def _lambda(lamv, lam_init):
    s1 = jnp.sum(lamv[0:1] * lamv[1:2], axis=1, keepdims=True)
    s2 = jnp.sum(lamv[2:3] * lamv[3:4], axis=1, keepdims=True)
    return jnp.exp(s1) - jnp.exp(s2) + lam_init


def _fox_out(acc, l, r):
    o = acc / l
    lane = lax.broadcasted_iota(jnp.int32, (r, LANES), 1)
    return jnp.where(lane < LANES // 2, o[:r], o[r:])


def _diff_out(acc, l, r, lamv, g, lam_init):
    o = acc / l
    od = o[:r] - _lambda(lamv, lam_init) * o[r:]
    return _rms(od, g) * (1.0 - lam_init)


def _attn_prompt_kernel(*refs, is_fox, tq, lam_init):
    if is_fox:
        q_ref, k_ref, v_ref, c_ref, o_ref = refs
    else:
        q_ref, k_ref, v_ref, lamv_ref, g_ref, o_ref = refs
    qi = pl.program_id(2)
    q2 = _stack_masked(q_ref[0])

    def scores(kb):
        start = pl.multiple_of(kb * tq, tq)
        s = _dot_nt(q2, k_ref[0, pl.ds(start, tq), :])
        if is_fox:
            c0 = c_ref[0, 0, 0:1, pl.ds(start, tq)]
            c1 = c_ref[0, 0, 1:2, pl.ds(start, tq)]
            s = jnp.concatenate([s[:tq] - c0, s[tq:] - c1], axis=0)
        return s, v_ref[0, pl.ds(start, tq), :]

    def body(kb, carry):
        s, v = scores(kb)
        return _flash_update(s, v, *carry)

    init = (jnp.full((2 * tq, 1), NEG, F32), jnp.zeros((2 * tq, 1), F32), jnp.zeros((2 * tq, LANES), F32))
    carry = lax.fori_loop(0, qi, body, init)
    s, v = scores(qi)
    row = lax.broadcasted_iota(jnp.int32, (2 * tq, tq), 0)
    col = lax.broadcasted_iota(jnp.int32, (2 * tq, tq), 1)
    s = jnp.where(col <= jnp.where(row >= tq, row - tq, row), s, NEG)
    _, l, acc = _flash_update(s, v, *carry)
    if is_fox:
        o_ref[0] = _fox_out(acc, l, tq).astype(BF16)
    else:
        o_ref[0] = _diff_out(acc, l, tq, lamv_ref[...], g_ref[...], lam_init).astype(BF16)


def _attn_prompt(q, k, v, extra, is_fox, tq, lam_init):
    b, t, _ = q.shape
    nblk = D_FOX // LANES
    qspec = pl.BlockSpec((1, tq, LANES), lambda bi, j, qi: (bi, qi, j))
    kvspec = pl.BlockSpec((1, t, LANES), lambda bi, j, qi: (bi, 0, j))
    if is_fox:
        especs = [pl.BlockSpec((1, 1, 2, t), lambda bi, j, qi: (bi, j, 0, 0))]
    else:
        especs = [_full((4, DH_DIFF)), _full((1, LANES))]
    return pl.pallas_call(
        functools.partial(_attn_prompt_kernel, is_fox=is_fox, tq=tq, lam_init=lam_init),
        grid=(b, nblk, t // tq),
        in_specs=[qspec, kvspec, kvspec] + especs,
        out_specs=qspec,
        out_shape=jax.ShapeDtypeStruct((b, t, D_FOX), BF16),
        compiler_params=_params(3),
        name="attn_prompt_fox" if is_fox else "attn_prompt_diff",
    )(q, k, v, *extra)


def _attn_decode_kernel(pt_ref, *refs, g_pages, n_steps, t_new, lam_init):
    n_in = 6 + 5 * g_pages + 5
    (qf_ref, qd_ref, kfn_ref, vfn_ref, kdn_ref, vdn_ref) = refs[:6]
    pages = refs[6:6 + 5 * g_pages]
    fk_refs, fv_refs, lc_refs, dk_refs, dv_refs = (pages[i * g_pages:(i + 1) * g_pages] for i in range(5))
    logfn_ref, lamv_ref, g_ref = refs[6 + 5 * g_pages:6 + 5 * g_pages + 3]
    o_ref = refs[n_in - 2]
    q2_s, m_s, l_s, acc_s, cc_s = refs[n_in - 1:]
    del pt_ref
    p = pl.program_id(1)
    nfb = D_FOX // LANES
    ndb = D_DIFF // LANES
    r = SUBLANES

    @pl.when(p == 0)
    def _():
        qf = qf_ref[0].astype(F32)
        qd = qd_ref[0].astype(F32)
        for j in range(nfb):
            q2_s[j] = _stack_masked(qf[:, j * LANES:(j + 1) * LANES]).astype(BF16)
        for j in range(ndb):
            q2_s[nfb + j] = _stack_masked(qd[:, j * LANES:(j + 1) * LANES]).astype(BF16)
        m_s[...] = jnp.full(m_s.shape, NEG, F32)
        l_s[...] = jnp.zeros(l_s.shape, F32)
        acc_s[...] = jnp.zeros(acc_s.shape, F32)
        cc_s[...] = jnp.zeros(cc_s.shape, F32)

    def update(j, u_parts, v_parts):
        m = m_s[j]
        m_new = m
        for u in u_parts:
            m_new = jnp.maximum(m_new, jnp.max(u, axis=1, keepdims=True))
        alpha = jnp.exp(m - m_new)
        l_new = alpha * l_s[j]
        acc_new = alpha * acc_s[j]
        for u, v in zip(u_parts, v_parts):
            pe = jnp.exp(u - m_new)
            l_new = l_new + jnp.sum(pe, axis=1, keepdims=True)
            acc_new = acc_new + _dot(pe.astype(BF16), v)
        m_s[j] = m_new
        l_s[j] = l_new
        acc_s[j] = acc_new

    @pl.when(p < n_steps)
    def _():
        cks = []
        run = cc_s[...]
        for gi in range(g_pages):
            lc = lc_refs[gi][0]
            cks.append(run + lc)
            run = run + lc[:, LANES - 1:LANES]
        cc_s[...] = run
        for j in range(nfb):
            us, vs = [], []
            for gi in range(g_pages):
                k = fk_refs[gi][0, :, j * LANES:(j + 1) * LANES].astype(BF16)
                s = _dot_nt(q2_s[j], k)
                ck = cks[gi]
                us.append(jnp.concatenate([s[:r] - ck[2 * j:2 * j + 1], s[r:] - ck[2 * j + 1:2 * j + 2]], axis=0))
                vs.append(fv_refs[gi][0, :, j * LANES:(j + 1) * LANES].astype(BF16))
            update(j, us, vs)
        for j in range(ndb):
            us, vs = [], []
            for gi in range(g_pages):
                k = dk_refs[gi][0, :, j * LANES:(j + 1) * LANES].astype(BF16)
                us.append(_dot_nt(q2_s[nfb + j], k))
                vs.append(dv_refs[gi][0, :, j * LANES:(j + 1) * LANES].astype(BF16))
            update(nfb + j, us, vs)

    @pl.when(p == n_steps)
    def _():
        row = lax.broadcasted_iota(jnp.int32, (2 * r, LANES), 0)
        col = lax.broadcasted_iota(jnp.int32, (2 * r, LANES), 1)
        keep = (col <= jnp.where(row >= r, row - r, row)) & (col < t_new)
        ck = cc_s[...] + _prefix_lanes(logfn_ref[0])
        lamv = lamv_ref[...]
        for j in range(nfb):
            s = _dot_nt(q2_s[j], kfn_ref[0, :, j * LANES:(j + 1) * LANES])
            u = jnp.concatenate([s[:r] - ck[2 * j:2 * j + 1], s[r:] - ck[2 * j + 1:2 * j + 2]], axis=0)
            update(j, [jnp.where(keep, u, NEG)], [vfn_ref[0, :, j * LANES:(j + 1) * LANES]])
            o_ref[0, :, j * LANES:(j + 1) * LANES] = _fox_out(acc_s[j], l_s[j], r).astype(BF16)
        for j in range(ndb):
            s = _dot_nt(q2_s[nfb + j], kdn_ref[0, :, j * LANES:(j + 1) * LANES])
            update(nfb + j, [jnp.where(keep, s, NEG)], [vdn_ref[0, :, j * LANES:(j + 1) * LANES]])
            o = _diff_out(acc_s[nfb + j], l_s[nfb + j], r, lamv, g_ref[...], lam_init)
            o_ref[0, :, D_FOX + j * LANES:D_FOX + (j + 1) * LANES] = o.astype(BF16)


def _attn_decode(page_table, qf, qd, kfn, vfn, kdn, vdn, fk, fv, lc, dk, dv, logfn, lamv, g, g_pages, lam_init):
    bs, t_new, _ = qf.shape
    n_pages = page_table.shape[1]
    n_steps = n_pages // g_pages
    nblk = (D_FOX + D_DIFF) // LANES

    def seq(shape):
        return pl.BlockSpec((1,) + shape, lambda b, p, pt: (b, 0, 0))

    def paged(shape, gi):
        def index(b, p, pt):
            return (pt[b, jnp.minimum(p, n_steps - 1) * g_pages + gi], 0, 0)
        return pl.BlockSpec((1,) + shape, index)

    in_specs = [seq((t_new, D_FOX)), seq((t_new, D_DIFF))] + [seq((PAGE_SIZE, D_FOX))] * 4
    args = [qf, qd, kfn, vfn, kdn, vdn]
    for arr, shape in ((fk, (PAGE_SIZE, D_FOX)), (fv, (PAGE_SIZE, D_FOX)), (lc, (H_FOX, LANES)),
                       (dk, (PAGE_SIZE, D_DIFF)), (dv, (PAGE_SIZE, D_DIFF))):
        for gi in range(g_pages):
            in_specs.append(paged(shape, gi))
            args.append(arr)
    in_specs += [seq((H_FOX, LANES)),
                 pl.BlockSpec((4, DH_DIFF), lambda b, p, pt: (0, 0)),
                 pl.BlockSpec((1, LANES), lambda b, p, pt: (0, 0))]
    args += [logfn, lamv, g]
    grid_spec = pltpu.PrefetchScalarGridSpec(
        num_scalar_prefetch=1,
        grid=(bs, n_steps + 1),
        in_specs=in_specs,
        out_specs=seq((t_new, D_MIX_AB)),
        scratch_shapes=[pltpu.VMEM((nblk, 2 * SUBLANES, LANES), BF16),
                        pltpu.VMEM((nblk, 2 * SUBLANES, 1), F32),
                        pltpu.VMEM((nblk, 2 * SUBLANES, 1), F32),
                        pltpu.VMEM((nblk, 2 * SUBLANES, LANES), F32),
                        pltpu.VMEM((H_FOX, LANES), F32)],
    )
    return pl.pallas_call(
        functools.partial(_attn_decode_kernel, g_pages=g_pages, n_steps=n_steps, t_new=t_new, lam_init=lam_init),
        grid_spec=grid_spec,
        out_shape=jax.ShapeDtypeStruct((bs, t_new, D_MIX_AB), BF16),
        compiler_params=_params(2),
        name="attn_decode",
    )(page_table, *args)


def _ffn_kernel(*refs, tm, tf, tiles_per_seq, paged_prev, final_norm):
    it = iter(refs)
    x_ref, mix_ref, wo_ref, g_ref, wa_ref, wb_ref, cw_ref, cb_ref, wd_ref = (next(it) for _ in range(9))
    if paged_prev:
        p1_ref, p2_ref = next(it), next(it)
    if final_norm:
        fg_ref = next(it)
    o_ref, nb_ref = next(it), next(it)
    x1_s, h_s, acc_s = next(it), next(it), next(it)
    if not paged_prev:
        carry_s = next(it)
    mi = pl.program_id(0)
    f = pl.program_id(1)

    @pl.when(f == 0)
    def _():
        x1 = x_ref[...] + _dot(mix_ref[...], wo_ref[...])
        x1_s[...] = x1
        h_s[...] = _rms(x1, g_ref[...]).astype(BF16)
        acc_s[...] = jnp.zeros(acc_s.shape, F32)

    h = h_s[...]
    a = _dot(h, wa_ref[...])
    gate = _dot(h, wb_ref[...])
    row = lax.broadcasted_iota(jnp.int32, (tm, tf), 0)
    if paged_prev:
        t = row & (SUBLANES - 1)
        a1 = jnp.where(t == 0, p1_ref[...], pltpu.roll(a, 1, 0))
        a2 = jnp.where(t < 2, p2_ref[...], pltpu.roll(a, 2, 0))
        nb_ref[...] = a
    else:
        @pl.when((mi % tiles_per_seq) == 0)
        def _():
            carry_s[f] = jnp.zeros((SUBLANES, tf), F32)

        c = carry_s[f]
        a1 = jnp.where(row == 0, c[SUBLANES - 1:SUBLANES], pltpu.roll(a, 1, 0))
        a2 = jnp.where(row == 0, c[SUBLANES - 2:SUBLANES - 1],
                       jnp.where(row == 1, c[SUBLANES - 1:SUBLANES], pltpu.roll(a, 2, 0)))
        carry_s[f] = a[tm - SUBLANES:, :]
        nb_ref[0] = a[tm - (FFN_CONV_W - 1):, :]
    cw = cw_ref[...]
    ac = cb_ref[...] + a2 * cw[0:1] + a1 * cw[1:2] + a * cw[2:3]
    act = (_gelu(ac) * gate).astype(BF16)
    acc_s[...] += _dot(act, wd_ref[...])

    @pl.when(f == pl.num_programs(1) - 1)
    def _():
        out = x1_s[...] + acc_s[...]
        if final_norm:
            out = _rms(out, fg_ref[...])
        o_ref[...] = out


def _ffn(x2d, mix, wo, g, wa, wb, cw, cb, wd, b, t, tm, tf, prev=None, final_g=None):
    m = x2d.shape[0]
    kmix = mix.shape[1]
    paged_prev = prev is not None
    tiles_per_seq = max(t // tm, 1)
    rows = lambda n: pl.BlockSpec((tm, n), lambda i, f: (i, 0))
    cols = lambda r: pl.BlockSpec((r, tf), lambda i, f: (0, f))
    in_specs = [rows(D_MODEL), rows(kmix), _full((kmix, D_MODEL)), _full((1, D_MODEL)),
                cols(D_MODEL), cols(D_MODEL), cols(FFN_CONV_W), cols(1),
                pl.BlockSpec((tf, D_MODEL), lambda i, f: (f, 0))]
    args = [x2d, mix, wo, g, wa, wb, cw, cb, wd]
    scratch = [pltpu.VMEM((tm, D_MODEL), F32), pltpu.VMEM((tm, D_MODEL), BF16), pltpu.VMEM((tm, D_MODEL), F32)]
    if paged_prev:
        in_specs += [pl.BlockSpec((tm, tf), lambda i, f: (i, f))] * 2
        args += list(prev)
        nb_spec = pl.BlockSpec((tm, tf), lambda i, f: (i, f))
        nb_shape = jax.ShapeDtypeStruct((m, D_FF), F32)
    else:
        nb_spec = pl.BlockSpec((1, FFN_CONV_W - 1, tf), lambda i, f: (i, 0, f))
        nb_shape = jax.ShapeDtypeStruct((m // tm, FFN_CONV_W - 1, D_FF), F32)
        scratch.append(pltpu.VMEM((D_FF // tf, SUBLANES, tf), F32))
    if final_g is not None:
        in_specs.append(_full((1, D_MODEL)))
        args.append(final_g)
    return pl.pallas_call(
        functools.partial(_ffn_kernel, tm=tm, tf=tf, tiles_per_seq=tiles_per_seq,
                          paged_prev=paged_prev, final_norm=final_g is not None),
        grid=(m // tm, D_FF // tf),
        in_specs=in_specs,
        out_specs=(rows(D_MODEL), nb_spec),
        out_shape=(jax.ShapeDtypeStruct((m, D_MODEL), F32), nb_shape),
        scratch_shapes=scratch,
        compiler_params=_params(2),
        name="ffn",
    )(*args)


def _lru_kernel(*refs, tm, tiles_per_seq, paged_prev):
    it = iter(refs)
    x_ref, g_ref, wg_ref, wx_ref, cw_ref, cb_ref, wri_ref, bri_ref, lam_ref = (next(it) for _ in range(9))
    if paged_prev:
        p1_ref, p2_ref, p3_ref, h0_ref = (next(it) for _ in range(4))
    mix_ref, u_ref, hl_ref = next(it), next(it), next(it)
    a_s, bx_s, hs_s = next(it), next(it), next(it)
    if not paged_prev:
        cu_s, ch_s = next(it), next(it)
    mi = pl.program_id(0)

    h = _rms(x_ref[...], g_ref[...]).astype(BF16)
    u = _dot(h, wx_ref[...])
    row = lax.broadcasted_iota(jnp.int32, (tm, D_RNN), 0)
    if paged_prev:
        t = row & (SUBLANES - 1)
        u1 = jnp.where(t == 0, p1_ref[...], pltpu.roll(u, 1, 0))
        u2 = jnp.where(t < 2, p2_ref[...], pltpu.roll(u, 2, 0))
        u3 = jnp.where(t < 3, p3_ref[...], pltpu.roll(u, 3, 0))
        u_ref[...] = u
    else:
        @pl.when((mi % tiles_per_seq) == 0)
        def _():
            cu_s[...] = jnp.zeros(cu_s.shape, F32)
            ch_s[...] = jnp.zeros(ch_s.shape, F32)

        c = cu_s[...]
        l1, l2, l3 = (c[SUBLANES - k:SUBLANES - k + 1] for k in (1, 2, 3))
        u1 = jnp.where(row == 0, l1, pltpu.roll(u, 1, 0))
        u2 = jnp.where(row == 0, l2, jnp.where(row == 1, l1, pltpu.roll(u, 2, 0)))
        u3 = jnp.where(row == 0, l3, jnp.where(row == 1, l2, jnp.where(row == 2, l1, pltpu.roll(u, 3, 0))))
        cu_s[...] = u[tm - SUBLANES:, :]
        u_ref[0] = u[tm - (LRU_CONV_W - 1):, :]
    cw = cw_ref[...]
    xc = cb_ref[...] + u3 * cw[0:1] + u2 * cw[1:2] + u1 * cw[2:3] + u * cw[3:4]
    ri = _dot(xc.astype(BF16), wri_ref[...]) + bri_ref[...]
    rg = _sigmoid(ri[:, :D_RNN])
    ig = _sigmoid(ri[:, D_RNN:])
    log_a = LRU_C * rg * _log_sigmoid(lam_ref[...])
    av = jnp.exp(log_a)
    a_s[...] = av
    bx_s[...] = jnp.sqrt(-jnp.tanh(log_a) * (av * av + 1.0)) * (ig * xc)

    r8 = lax.broadcasted_iota(jnp.int32, (SUBLANES, D_RNN), 0)

    def group(gi, hc):
        sl = pl.ds(pl.multiple_of(gi * SUBLANES, SUBLANES), SUBLANES)
        av = a_s[sl, :]
        bv = bx_s[sl, :]
        for s in (1, 2, 4):
            keep = r8 >= s
            bv = jnp.where(keep, av * pltpu.roll(bv, s, 0) + bv, bv)
            av = jnp.where(keep, av * pltpu.roll(av, s, 0), av)
        if paged_prev:
            hc = h0_ref[pl.ds(gi, 1), :]
        hs = av * hc + bv
        hs_s[sl, :] = hs
        return hs[SUBLANES - 1:SUBLANES, :]

    if paged_prev:
        hc0 = jnp.zeros((1, D_RNN), F32)
    else:
        hc0 = ch_s[SUBLANES - 1:SUBLANES, :]
    lax.fori_loop(0, tm // SUBLANES, group, hc0)
    hs = hs_s[...]
    if paged_prev:
        hl_ref[...] = hs
    else:
        ch_s[...] = hs[tm - SUBLANES:, :]
        hl_ref[0] = hs[tm - 1:, :]
    gate = _gelu(_dot(h, wg_ref[...]))
    mix_ref[...] = (hs * gate).astype(BF16)


def _lru(x2d, g, wg, wx, cw, cb, wri, bri, lam, b, t, tm, prev=None):
    m = x2d.shape[0]
    paged_prev = prev is not None
    tiles_per_seq = max(t // tm, 1)
    rows = lambda n: pl.BlockSpec((tm, n), lambda i: (i, 0))
    in_specs = [rows(D_MODEL), _full((1, D_MODEL)), _full((D_MODEL, D_RNN)), _full((D_MODEL, D_RNN)),
                _full((LRU_CONV_W, D_RNN)), _full((1, D_RNN)), _full((D_RNN, 2 * D_RNN)), _full((1, 2 * D_RNN)),
                _full((1, D_RNN))]
    args = [x2d, g, wg, wx, cw, cb, wri, bri, lam]
    scratch = [pltpu.VMEM((tm, D_RNN), F32)] * 3
    if paged_prev:
        in_specs += [rows(D_RNN)] * 3 + [_full(prev[3].shape)]
        args += list(prev)
        u_spec, hl_spec = rows(D_RNN), rows(D_RNN)
        u_shape = hl_shape = jax.ShapeDtypeStruct((m, D_RNN), F32)
    else:
        u_spec = pl.BlockSpec((1, LRU_CONV_W - 1, D_RNN), lambda i: (i, 0, 0))
        hl_spec = pl.BlockSpec((1, 1, D_RNN), lambda i: (i, 0, 0))
        u_shape = jax.ShapeDtypeStruct((m // tm, LRU_CONV_W - 1, D_RNN), F32)
        hl_shape = jax.ShapeDtypeStruct((m // tm, 1, D_RNN), F32)
        scratch += [pltpu.VMEM((SUBLANES, D_RNN), F32)] * 2
    return pl.pallas_call(
        functools.partial(_lru_kernel, tm=tm, tiles_per_seq=tiles_per_seq, paged_prev=paged_prev),
        grid=(m // tm,),
        in_specs=in_specs,
        out_specs=(rows(D_RNN), u_spec, hl_spec),
        out_shape=(jax.ShapeDtypeStruct((m, D_RNN), BF16), u_shape, hl_shape),
        scratch_shapes=scratch,
        compiler_params=_params(1),
        name="lru",
    )(*args)


def _rope_tables(pos):
    half = ROT_DIMS // 2
    inv = ROPE_THETA ** (-jnp.arange(half, dtype=F32) / half)
    ang = pos.astype(F32)[:, None] * inv[None, :]
    lane = jnp.arange(LANES) % DH_DIFF
    cos = jnp.cos(ang)[:, lane % half]
    sin = jnp.sin(ang)[:, lane % half]
    cos_t = jnp.where(lane[None, :] < ROT_DIMS, cos, 1.0)
    sin_t = jnp.where(lane[None, :] < half, -sin, jnp.where(lane[None, :] < ROT_DIMS, sin, 0.0))
    return cos_t, sin_t


def _block_diag(w):
    n, c, d = w.shape
    idx = jnp.arange(n)
    return jnp.zeros((n, c, n, d), w.dtype).at[idx, :, idx, :].set(w).reshape(n * c, n * d)


def _expand_prev(buf, t, k):
    bs, wm1, c = buf.shape
    out = jnp.zeros((bs, t, c), buf.dtype)
    for step in range(min(k, t)):
        out = out.at[:, step].set(buf[:, wm1 - k + step])
    return out.reshape(bs * t, c)


def kernel(x_prompt, x_sample, cache_fox_k, cache_fox_v, cache_fox_logf, cache_diff_k, cache_diff_v, state_lru_conv, state_lru_h, state_ffn_conv, page_table, mix_norm_g, ab_w_in, ab_b_f, ab_lam_q1, ab_lam_k1, ab_lam_q2, ab_lam_k2, ab_subln_g, ab_w_out, lru_w_gate, lru_w_x, lru_conv_w, lru_conv_b, lru_w_a, lru_b_a, lru_w_i, lru_b_i, lru_lambda, lru_w_out, ffn_norm_g, ffn_w_a, ffn_w_b, ffn_conv_w, ffn_conv_b, ffn_w_down, final_norm_g):
    depth = mix_norm_g.shape[0]
    assert depth == 2 and ab_w_in.shape[0] == 1 and lru_w_x.shape[0] == 1

    w_in = ab_w_in[0]
    o_f = 3 * D_FOX
    w_all = jnp.concatenate([w_in[:, :o_f], w_in[:, o_f + H_FOX:], w_in[:, o_f:o_f + H_FOX],
                             jnp.zeros((D_MODEL, LANES - H_FOX), F32)], axis=1).astype(BF16)
    bf_pad = jnp.pad(ab_b_f[0], (0, LANES - H_FOX)).reshape(1, LANES)
    lamv = jnp.stack([ab_lam_q1[0], ab_lam_k1[0], ab_lam_q2[0], ab_lam_k2[0]]).astype(F32)
    subln_g = ab_subln_g[0].reshape(1, LANES)
    w_out = ab_w_out[0].astype(BF16)
    wg = lru_w_gate[0].astype(BF16)
    wx = lru_w_x[0].astype(BF16)
    wri = jnp.concatenate([_block_diag(lru_w_a[0]), _block_diag(lru_w_i[0])], axis=1).astype(BF16)
    bri = jnp.concatenate([lru_b_a[0], lru_b_i[0]]).reshape(1, 2 * D_RNN)
    lru_wo = lru_w_out[0].astype(BF16)
    ffn_wa = ffn_w_a.astype(BF16)
    ffn_wb = ffn_w_b.astype(BF16)
    ffn_wd = ffn_w_down.astype(BF16)
    lam_init = 0.8 - 0.6 * math.exp(-0.3 * 0)

    def layer0_proj(x2d, pos_rows, tm):
        cos_t, sin_t = _rope_tables(pos_rows)
        return _inproj(x2d, mix_norm_g[0].reshape(1, D_MODEL), w_all, bf_pad, cos_t, sin_t, tm)

    def tail(x2d, mix0, b, t, tm_ffn, tm_lru, tf, ffn_prev, lru_prev):
        x1, nb0 = _ffn(x2d, mix0, w_out, ffn_norm_g[0].reshape(1, D_MODEL), ffn_wa[0], ffn_wb[0],
                       ffn_conv_w[0], ffn_conv_b[0].reshape(1, D_FF), ffn_wd[0], b, t, tm_ffn, tf,
                       prev=ffn_prev[0])
        mix1, u_out, h_out = _lru(x1, mix_norm_g[1].reshape(1, D_MODEL), wg, wx, lru_conv_w[0],
                                  lru_conv_b[0].reshape(1, D_RNN), wri, bri, lru_lambda[0].reshape(1, D_RNN),
                                  b, t, tm_lru, prev=lru_prev)
        y, nb1 = _ffn(x1, mix1, lru_wo, ffn_norm_g[1].reshape(1, D_MODEL), ffn_wa[1], ffn_wb[1],
                      ffn_conv_w[1], ffn_conv_b[1].reshape(1, D_FF), ffn_wd[1], b, t, tm_ffn, tf,
                      prev=ffn_prev[1], final_g=final_norm_g.reshape(1, D_MODEL))
        return y, nb0, nb1, u_out, h_out

    bp, tp, _ = x_prompt.shape
    mp = bp * tp
    xp = x_prompt.reshape(mp, D_MODEL)
    (qf, kf, vf, kfb, vfb, logf, logft, qd, kd, vd, kdb, vdb) = layer0_proj(xp, jnp.arange(tp), 512)
    cum = _cumsum_prompt(logft, bp, tp)
    r3 = lambda a: a.reshape(bp, tp, D_FOX)
    tq = 512
    of = _attn_prompt(r3(qf), r3(kfb), r3(vfb), (cum,), True, tq, lam_init)
    od = _attn_prompt(r3(qd), r3(kdb), r3(vdb), (lamv, subln_g), False, tq, lam_init)
    mix0 = jnp.concatenate([of, od], axis=-1).reshape(mp, D_MIX_AB)
    tm_ffn, tm_lru = 512, 256
    y_p, nb0, nb1, u_out, h_out = tail(xp, mix0, bp, tp, tm_ffn, tm_lru, 512, (None, None), None)
    seq_last = lambda a, tm: a[tp // tm - 1::tp // tm]
    nb0, nb1, u_out, h_out = seq_last(nb0, tm_ffn), seq_last(nb1, tm_ffn), seq_last(u_out, tm_lru), seq_last(h_out, tm_lru)
    y_prompt = y_p.reshape(bp, tp, D_MODEL)
    p_state = (kf.reshape(1, bp, tp, H_FOX, DH_FOX), vf.reshape(1, bp, tp, H_FOX, DH_FOX),
               logf.reshape(1, bp, tp, H_FOX),
               kd.reshape(1, bp, tp, H_DIFF, 2 * DH_DIFF), vd.reshape(1, bp, tp, H_DIFF, 2 * DH_DIFF),
               u_out[None], h_out.reshape(1, bp, D_RNN), jnp.stack([nb0, nb1]))

    bs, ts, _ = x_sample.shape
    assert ts == SUBLANES
    ms = bs * ts
    n_pages = page_table.shape[1]
    past = n_pages * PAGE_SIZE
    xs = x_sample.reshape(ms, D_MODEL)
    pos_rows = jnp.tile(past + jnp.arange(ts), bs)
    (qf, kf, vf, kfb, vfb, logf, logft, qd, kd, vd, kdb, vdb) = layer0_proj(xs, pos_rows, ms)
    n_pool = cache_fox_k.shape[1]
    logf_pool = jnp.swapaxes(cache_fox_logf[0], 1, 2).reshape(n_pool * H_FOX, PAGE_SIZE)
    lc = _cumsum_rows(logf_pool).reshape(n_pool, H_FOX, PAGE_SIZE)
    logfn = jnp.pad(logft.reshape(H_FOX, bs, ts).transpose(1, 0, 2), ((0, 0), (0, 0), (0, LANES - ts)))
    pad_rows = lambda a: jnp.pad(a.reshape(bs, ts, D_FOX), ((0, 0), (0, PAGE_SIZE - ts), (0, 0)))
    mix0 = _attn_decode(page_table, qf.reshape(bs, ts, D_FOX), qd.reshape(bs, ts, D_DIFF),
                        pad_rows(kfb), pad_rows(vfb), pad_rows(kdb), pad_rows(vdb),
                        cache_fox_k[0].reshape(n_pool, PAGE_SIZE, D_FOX),
                        cache_fox_v[0].reshape(n_pool, PAGE_SIZE, D_FOX), lc,
                        cache_diff_k[0].reshape(n_pool, PAGE_SIZE, D_DIFF),
                        cache_diff_v[0].reshape(n_pool, PAGE_SIZE, D_DIFF),
                        logfn, lamv, subln_g, 4, lam_init).reshape(ms, D_MIX_AB)
    ffn_prev = tuple((_expand_prev(state_ffn_conv[l], ts, 1), _expand_prev(state_ffn_conv[l], ts, 2))
                     for l in range(depth))
    lru_prev = tuple(_expand_prev(state_lru_conv[0], ts, k) for k in (1, 2, 3)) + (state_lru_h[0],)
    y_s, a0, a1, u_full, h_full = tail(xs, mix0, bs, ts, ms, ms, 512, ffn_prev, lru_prev)
    y_sample = y_s.reshape(bs, ts, D_MODEL)
    last = lambda a, k: a.reshape(bs, ts, -1)[:, ts - k:]
    s_state = (kf.reshape(1, bs, ts, H_FOX, DH_FOX), vf.reshape(1, bs, ts, H_FOX, DH_FOX),
               logf.reshape(1, bs, ts, H_FOX),
               kd.reshape(1, bs, ts, H_DIFF, 2 * DH_DIFF), vd.reshape(1, bs, ts, H_DIFF, 2 * DH_DIFF),
               last(u_full, LRU_CONV_W - 1)[None], last(h_full, 1).reshape(1, bs, D_RNN),
               jnp.stack([last(a0, FFN_CONV_W - 1), last(a1, FFN_CONV_W - 1)]))

    return (y_prompt, y_sample) + p_state + s_state
```

```python
import functools
import math

import jax
import jax.numpy as jnp
from jax import lax
from jax.experimental import pallas as pl
from jax.experimental.pallas import tpu as pltpu

F32 = jnp.float32
BF16 = jnp.bfloat16

D_MODEL = 1024
H_FOX = 8
DH_FOX = 64
H_DIFF = 4
DH_DIFF = 64
D_FOX = H_FOX * DH_FOX
D_DIFF = H_DIFF * 2 * DH_DIFF
D_MIX_AB = D_FOX + D_DIFF
ROT_DIMS = DH_DIFF // 4
ROPE_THETA = 500000.0
D_RNN = 1280
N_LRU_BLOCKS = 16
LRU_BW = D_RNN // N_LRU_BLOCKS
LRU_CONV_W = 4
LRU_C = 8.0
D_FF = 3072
FFN_CONV_W = 3
EPS = 1e-6
NEG = -1e30
PAGE_SIZE = 128

LANES = 128
SUBLANES = 8
VMEM_LIMIT = 56 * 1024 * 1024
N_QKV = 6 * D_FOX
N_INPROJ = N_QKV + LANES


def _params(n_axes):
    return pltpu.CompilerParams(dimension_semantics=("arbitrary",) * n_axes,
                                vmem_limit_bytes=VMEM_LIMIT)


def _full(shape):
    n = len(shape)
    return pl.BlockSpec(shape, lambda *_: (0,) * n)


def _rms(x, g):
    return x * lax.rsqrt(jnp.mean(x * x, axis=-1, keepdims=True) + EPS) * g


def _gelu(x):
    c = math.sqrt(2.0 / math.pi)
    return x * (0.5 * (1.0 + jnp.tanh(c * (x + 0.044715 * (x * x * x)))))


def _log_sigmoid(z):
    return -(jnp.maximum(-z, 0.0) + jnp.log1p(jnp.exp(-jnp.abs(z))))


def _sigmoid(z):
    return 1.0 / (1.0 + jnp.exp(-z))


def _dot(a, b):
    return jnp.dot(a, b, preferred_element_type=F32)


def _dot_nt(a, b):
    return lax.dot_general(a, b, (((1,), (1,)), ((), ())), preferred_element_type=F32)


def _prefix_lanes(x):
    r = lax.broadcasted_iota(jnp.int32, (LANES, LANES), 0)
    c = lax.broadcasted_iota(jnp.int32, (LANES, LANES), 1)
    tri = jnp.where(r <= c, 1.0, 0.0).astype(BF16)
    hi = x.astype(BF16)
    r1 = x - hi.astype(F32)
    mid = r1.astype(BF16)
    lo = (r1 - mid.astype(F32)).astype(BF16)
    return _dot(hi, tri) + _dot(mid, tri) + _dot(lo, tri)


def _inproj_kernel(x_ref, g_ref, w_ref, bf_ref, cos_ref, sin_ref,
                   qf_ref, kf_ref, vf_ref, kfb_ref, vfb_ref, logf_ref, logft_ref,
                   qd_ref, kd_ref, vd_ref, kdb_ref, vdb_ref):
    h = _rms(x_ref[...], g_ref[...]).astype(BF16)

    def proj(i, width=D_FOX):
        return _dot(h, w_ref[:, i * D_FOX:i * D_FOX + width])

    scale = DH_FOX ** -0.5
    qf_ref[...] = (proj(0) * scale).astype(BF16)
    kf = proj(1)
    kf_ref[...] = kf
    kfb_ref[...] = kf.astype(BF16)
    vf = proj(2)
    vf_ref[...] = vf
    vfb_ref[...] = vf.astype(BF16)

    logf = _log_sigmoid(proj(6, LANES) + bf_ref[...])
    logf_ref[...] = logf[:, :H_FOX]
    logft_ref[...] = logf.T[:H_FOX, :]

    cosf = jnp.concatenate([cos_ref[...]] * (D_DIFF // LANES), axis=1)
    sinf = jnp.concatenate([sin_ref[...]] * (D_DIFF // LANES), axis=1)
    lane = lax.broadcasted_iota(jnp.int32, cosf.shape, 1) & (DH_DIFF - 1)
    half = ROT_DIMS // 2

    def rope(x):
        partner = jnp.where(lane < half, pltpu.roll(x, D_DIFF - half, 1), pltpu.roll(x, half, 1))
        return jnp.where(lane < ROT_DIMS, x * cosf + partner * sinf, x)

    qd_ref[...] = (rope(proj(3)) * (DH_DIFF ** -0.5)).astype(BF16)
    kd = rope(proj(4))
    kd_ref[...] = kd
    kdb_ref[...] = kd.astype(BF16)
    vd = proj(5)
    vd_ref[...] = vd
    vdb_ref[...] = vd.astype(BF16)


def _inproj(x2d, g, w_all, bf_pad, cos_t, sin_t, tm):
    m = x2d.shape[0]
    nt = cos_t.shape[0] // tm
    row = lambda n: pl.BlockSpec((tm, n), lambda i: (i, 0))
    tab = pl.BlockSpec((tm, LANES), lambda i: (i % nt, 0))
    f32o = jax.ShapeDtypeStruct((m, D_FOX), F32)
    b16o = jax.ShapeDtypeStruct((m, D_FOX), BF16)
    out_shape = (b16o, f32o, f32o, b16o, b16o,
                 jax.ShapeDtypeStruct((m, H_FOX), F32), jax.ShapeDtypeStruct((H_FOX, m), F32),
                 b16o, f32o, f32o, b16o, b16o)
    out_specs = (row(D_FOX),) * 5 + (row(H_FOX), pl.BlockSpec((H_FOX, tm), lambda i: (0, i))) + (row(D_FOX),) * 5
    return pl.pallas_call(
        _inproj_kernel,
        grid=(m // tm,),
        in_specs=[row(D_MODEL), _full((1, D_MODEL)), _full((D_MODEL, N_INPROJ)), _full((1, LANES)), tab, tab],
        out_specs=out_specs,
        out_shape=out_shape,
        compiler_params=_params(1),
        name="inproj",
    )(x2d, g, w_all, bf_pad, cos_t, sin_t)


def _cumsum_prompt_kernel(x_ref, o_ref, *, t):
    carry = jnp.zeros((H_FOX, 1), F32)
    for i in range(t // LANES):
        c = _prefix_lanes(x_ref[:, i * LANES:(i + 1) * LANES]) + carry
        for j in range(H_FOX // 2):
            o_ref[0, j, :, i * LANES:(i + 1) * LANES] = c[2 * j:2 * j + 2, :]
        carry = c[:, LANES - 1:LANES]


def _cumsum_prompt(logft, b, t):
    return pl.pallas_call(
        functools.partial(_cumsum_prompt_kernel, t=t),
        grid=(b,),
        in_specs=[pl.BlockSpec((H_FOX, t), lambda i: (0, i))],
        out_specs=pl.BlockSpec((1, H_FOX // 2, 2, t), lambda i: (i, 0, 0, 0)),
        out_shape=jax.ShapeDtypeStruct((b, H_FOX // 2, 2, t), F32),
        compiler_params=_params(1),
        name="cumsum_prompt",
    )(logft)


def _cumsum_rows_kernel(x_ref, o_ref):
    o_ref[...] = _prefix_lanes(x_ref[...])


def _cumsum_rows(x, max_rows=2048):
    rows = x.shape[0]
    tr = max(d for d in range(SUBLANES, max_rows + 1, SUBLANES) if rows % d == 0)
    spec = pl.BlockSpec((tr, LANES), lambda i: (i, 0))
    return pl.pallas_call(
        _cumsum_rows_kernel,
        grid=(rows // tr,),
        in_specs=[spec],
        out_specs=spec,
        out_shape=jax.ShapeDtypeStruct((rows, LANES), F32),
        compiler_params=_params(1),
        name="cumsum_pages",
    )(x)


def _stack_masked(q):
    lane = lax.broadcasted_iota(jnp.int32, q.shape, 1)
    zero = jnp.zeros_like(q)
    return jnp.concatenate([jnp.where(lane < LANES // 2, q, zero),
                            jnp.where(lane >= LANES // 2, q, zero)], axis=0)


def _flash_update(u, v, m, l, acc):
    m_new = jnp.maximum(m, jnp.max(u, axis=1, keepdims=True))
    alpha = jnp.exp(m - m_new)
    p = jnp.exp(u - m_new)
    l_new = alpha * l + jnp.sum(p, axis=1, keepdims=True)
    acc_new = alpha * acc + _dot(p.astype(BF16), v)
    return m_new, l_new, acc_new


def _lambda(lamv, lam_init):
    s1 = jnp.sum(lamv[0:1] * lamv[1:2], axis=1, keepdims=True)
    s2 = jnp.sum(lamv[2:3] * lamv[3:4], axis=1, keepdims=True)
    return jnp.exp(s1) - jnp.exp(s2) + lam_init


def _fox_out(acc, l, r):
    o = acc / l
    lane = lax.broadcasted_iota(jnp.int32, (r, LANES), 1)
    return jnp.where(lane < LANES // 2, o[:r], o[r:])


def _diff_out(acc, l, r, lamv, g, lam_init):
    o = acc / l
    od = o[:r] - _lambda(lamv, lam_init) * o[r:]
    return _rms(od, g) * (1.0 - lam_init)


def _attn_prompt_kernel(*refs, is_fox, tq, lam_init):
    if is_fox:
        q_ref, k_ref, v_ref, c_ref, o_ref = refs
    else:
        q_ref, k_ref, v_ref, lamv_ref, g_ref, o_ref = refs
    qi = pl.program_id(2)
    q2 = _stack_masked(q_ref[0])

    def scores(kb):
        start = pl.multiple_of(kb * tq, tq)
        s = _dot_nt(q2, k_ref[0, pl.ds(start, tq), :])
        if is_fox:
            c0 = c_ref[0, 0, 0:1, pl.ds(start, tq)]
            c1 = c_ref[0, 0, 1:2, pl.ds(start, tq)]
            s = jnp.concatenate([s[:tq] - c0, s[tq:] - c1], axis=0)
        return s, v_ref[0, pl.ds(start, tq), :]

    def body(kb, carry):
        s, v = scores(kb)
        return _flash_update(s, v, *carry)

    init = (jnp.full((2 * tq, 1), NEG, F32), jnp.zeros((2 * tq, 1), F32), jnp.zeros((2 * tq, LANES), F32))
    carry = lax.fori_loop(0, qi, body, init)
    s, v = scores(qi)
    row = lax.broadcasted_iota(jnp.int32, (2 * tq, tq), 0)
    col = lax.broadcasted_iota(jnp.int32, (2 * tq, tq), 1)
    s = jnp.where(col <= jnp.where(row >= tq, row - tq, row), s, NEG)
    _, l, acc = _flash_update(s, v, *carry)
    if is_fox:
        o_ref[0] = _fox_out(acc, l, tq).astype(BF16)
    else:
        o_ref[0] = _diff_out(acc, l, tq, lamv_ref[...], g_ref[...], lam_init).astype(BF16)


def _attn_prompt(q, k, v, extra, is_fox, tq, lam_init):
    b, t, _ = q.shape
    nblk = D_FOX // LANES
    qspec = pl.BlockSpec((1, tq, LANES), lambda bi, j, qi: (bi, qi, j))
    kvspec = pl.BlockSpec((1, t, LANES), lambda bi, j, qi: (bi, 0, j))
    if is_fox:
        especs = [pl.BlockSpec((1, 1, 2, t), lambda bi, j, qi: (bi, j, 0, 0))]
    else:
        especs = [_full((4, DH_DIFF)), _full((1, LANES))]
    return pl.pallas_call(
        functools.partial(_attn_prompt_kernel, is_fox=is_fox, tq=tq, lam_init=lam_init),
        grid=(b, nblk, t // tq),
        in_specs=[qspec, kvspec, kvspec] + especs,
        out_specs=qspec,
        out_shape=jax.ShapeDtypeStruct((b, t, D_FOX), BF16),
        compiler_params=_params(3),
        name="attn_prompt_fox" if is_fox else "attn_prompt_diff",
    )(q, k, v, *extra)


def _attn_decode_kernel(pt_ref, *refs, g_pages, n_steps, t_new, lam_init):
    n_in = 6 + 5 * g_pages + 5
    (qf_ref, qd_ref, kfn_ref, vfn_ref, kdn_ref, vdn_ref) = refs[:6]
    pages = refs[6:6 + 5 * g_pages]
    fk_refs, fv_refs, lc_refs, dk_refs, dv_refs = (pages[i * g_pages:(i + 1) * g_pages] for i in range(5))
    logfn_ref, lamv_ref, g_ref = refs[6 + 5 * g_pages:6 + 5 * g_pages + 3]
    o_ref = refs[n_in - 2]
    q2_s, m_s, l_s, acc_s, cc_s = refs[n_in - 1:]
    del pt_ref
    p = pl.program_id(1)
    nfb = D_FOX // LANES
    ndb = D_DIFF // LANES
    r = SUBLANES

    nblk = nfb + ndb
    rb = 2 * r

    def blk(x, j):
        return x[j * rb:(j + 1) * rb]

    def lanes(x, j):
        return x[:, j * LANES:(j + 1) * LANES]

    @pl.when(p == 0)
    def _():
        q = jnp.concatenate([qf_ref[0], qd_ref[0]], axis=1)
        q2_s[...] = jnp.concatenate([_stack_masked(lanes(q, j)) for j in range(nblk)], axis=0).astype(BF16)
        m_s[...] = jnp.full(m_s.shape, NEG, F32)
        l_s[...] = jnp.zeros(l_s.shape, F32)
        acc_s[...] = jnp.zeros(acc_s.shape, F32)
        cc_s[...] = jnp.zeros(cc_s.shape, F32)

    def fox_bias(ck):
        return jnp.concatenate([jnp.broadcast_to(ck[h:h + 1], (r, ck.shape[1])) for h in range(H_FOX)], axis=0)

    nf = nfb * rb
    nd = ndb * rb

    def update(lo, n, u, pv_fn):
        sl = slice(lo, lo + n)
        m = m_s[sl]
        m_new = jnp.maximum(m, jnp.max(u, axis=1, keepdims=True))
        alpha = jnp.exp(m - m_new)
        pe = jnp.exp(u - m_new)
        m_s[sl] = m_new
        l_s[sl] = alpha * l_s[sl] + jnp.sum(pe, axis=1, keepdims=True)
        acc_s[sl] = alpha * acc_s[sl] + pv_fn(pe.astype(BF16))

    @pl.when(p < n_steps)
    def _():
        cks = []
        run = cc_s[...]
        for gi in range(g_pages):
            lc = lc_refs[gi][0]
            cks.append(run + lc)
            run = run + lc[:, LANES - 1:LANES]
        cc_s[...] = run
        q2 = q2_s[...]

        def fkv(refs, gi, j):
            return refs[gi][0, j * LANES:(j + 1) * LANES, :].astype(BF16)

        s_fox = jnp.concatenate(
            [jnp.concatenate([_dot(blk(q2, j), fkv(fk_refs, gi, j)) for gi in range(g_pages)], axis=1)
             for j in range(nfb)], axis=0)

        def pv_fox(pb):
            return jnp.concatenate(
                [sum(_dot_nt(lanes(blk(pb, j), gi), fkv(fv_refs, gi, j)) for gi in range(g_pages))
                 for j in range(nfb)], axis=0)

        update(0, nf, s_fox - fox_bias(jnp.concatenate(cks, axis=1)), pv_fox)

        wd = PAGE_SIZE * H_DIFF
        s_diff = jnp.concatenate([_dot_nt(q2[nf:], dk_refs[gi][0].astype(BF16)) for gi in range(g_pages)], axis=1)
        row = lax.broadcasted_iota(jnp.int32, s_diff.shape, 0)
        col = lax.broadcasted_iota(jnp.int32, s_diff.shape, 1)
        own = (col & (H_DIFF - 1)) == row // rb

        def pv_diff(pb):
            return sum(_dot(pb[:, gi * wd:(gi + 1) * wd], dv_refs[gi][0].astype(BF16)) for gi in range(g_pages))

        update(nf, nd, jnp.where(own, s_diff, NEG), pv_diff)

    @pl.when(p == n_steps)
    def _():
        row = lax.broadcasted_iota(jnp.int32, (nf, LANES), 0) & (r - 1)
        col = lax.broadcasted_iota(jnp.int32, (nf, LANES), 1)
        keep = (col <= row) & (col < t_new)
        ck = cc_s[...] + _prefix_lanes(logfn_ref[0])
        q2 = q2_s[...]
        s_fox = jnp.concatenate([_dot_nt(blk(q2, j), lanes(kfn_ref[0], j)) for j in range(nfb)], axis=0)
        update(0, nf, jnp.where(keep, s_fox - fox_bias(ck), NEG),
               lambda pb: jnp.concatenate([_dot(blk(pb, j), lanes(vfn_ref[0], j)) for j in range(nfb)], axis=0))
        s_diff = jnp.concatenate([_dot_nt(blk(q2, nfb + j), lanes(kdn_ref[0], j)) for j in range(ndb)], axis=0)
        update(nf, nd, jnp.where(keep, s_diff, NEG),
               lambda pb: jnp.concatenate([_dot(blk(pb, j), lanes(vdn_ref[0], j)) for j in range(ndb)], axis=0))
        acc = acc_s[...]
        l = l_s[...]
        lamv = lamv_ref[...]
        for j in range(nfb):
            o_ref[0, :, j * LANES:(j + 1) * LANES] = _fox_out(blk(acc, j), blk(l, j), r).astype(BF16)
        for j in range(nfb, nblk):
            o = _diff_out(blk(acc, j), blk(l, j), r, lamv, g_ref[...], lam_init)
            o_ref[0, :, j * LANES:(j + 1) * LANES] = o.astype(BF16)


def _attn_decode(page_table, qf, qd, kfn, vfn, kdn, vdn, fk, fv, lc, dk, dv, logfn, lamv, g, g_pages, lam_init):
    bs, t_new, _ = qf.shape
    n_pages = page_table.shape[1]
    n_steps = n_pages // g_pages
    nblk = (D_FOX + D_DIFF) // LANES

    def seq(shape):
        return pl.BlockSpec((1,) + shape, lambda b, p, pt: (b, 0, 0))

    def paged(shape, gi):
        def index(b, p, pt):
            return (pt[b, jnp.minimum(p, n_steps - 1) * g_pages + gi],) + (0,) * len(shape)
        return pl.BlockSpec((1,) + shape, index)

    in_specs = [seq((t_new, D_FOX)), seq((t_new, D_DIFF))] + [seq((PAGE_SIZE, D_FOX))] * 4
    args = [qf, qd, kfn, vfn, kdn, vdn]
    for arr in (fk, fv, lc, dk, dv):
        for gi in range(g_pages):
            in_specs.append(paged(arr.shape[1:], gi))
            args.append(arr)
    in_specs += [seq((H_FOX, LANES)),
                 pl.BlockSpec((4, DH_DIFF), lambda b, p, pt: (0, 0)),
                 pl.BlockSpec((1, LANES), lambda b, p, pt: (0, 0))]
    args += [logfn, lamv, g]
    grid_spec = pltpu.PrefetchScalarGridSpec(
        num_scalar_prefetch=1,
        grid=(bs, n_steps + 1),
        in_specs=in_specs,
        out_specs=seq((t_new, D_MIX_AB)),
        scratch_shapes=[pltpu.VMEM((nblk * 2 * t_new, LANES), BF16),
                        pltpu.VMEM((nblk * 2 * t_new, 1), F32),
                        pltpu.VMEM((nblk * 2 * t_new, 1), F32),
                        pltpu.VMEM((nblk * 2 * t_new, LANES), F32),
                        pltpu.VMEM((H_FOX, LANES), F32)],
    )
    return pl.pallas_call(
        functools.partial(_attn_decode_kernel, g_pages=g_pages, n_steps=n_steps, t_new=t_new, lam_init=lam_init),
        grid_spec=grid_spec,
        out_shape=jax.ShapeDtypeStruct((bs, t_new, D_MIX_AB), BF16),
        compiler_params=_params(2),
        name="attn_decode",
    )(page_table, *args)


def _ffn_kernel(*refs, tm, tf, tiles_per_seq, paged_prev, final_norm):
    it = iter(refs)
    x_ref, mix_ref, wo_ref, g_ref, wa_ref, wb_ref, cw_ref, cb_ref, wd_ref = (next(it) for _ in range(9))
    if paged_prev:
        p1_ref, p2_ref = next(it), next(it)
    if final_norm:
        fg_ref = next(it)
    o_ref, nb_ref = next(it), next(it)
    x1_s, h_s, acc_s = next(it), next(it), next(it)
    if not paged_prev:
        carry_s = next(it)
    mi = pl.program_id(0)
    f = pl.program_id(1)

    @pl.when(f == 0)
    def _():
        x1 = x_ref[...] + _dot(mix_ref[...], wo_ref[...])
        x1_s[...] = x1
        h_s[...] = _rms(x1, g_ref[...]).astype(BF16)
        acc_s[...] = jnp.zeros(acc_s.shape, F32)

    h = h_s[...]
    a = _dot(h, wa_ref[...])
    gate = _dot(h, wb_ref[...])
    row = lax.broadcasted_iota(jnp.int32, (tm, tf), 0)
    if paged_prev:
        t = row & (SUBLANES - 1)
        a1 = jnp.where(t == 0, p1_ref[...], pltpu.roll(a, 1, 0))
        a2 = jnp.where(t < 2, p2_ref[...], pltpu.roll(a, 2, 0))
        nb_ref[...] = a
    else:
        @pl.when((mi % tiles_per_seq) == 0)
        def _():
            carry_s[f] = jnp.zeros((SUBLANES, tf), F32)

        c = carry_s[f]
        a1 = jnp.where(row == 0, c[SUBLANES - 1:SUBLANES], pltpu.roll(a, 1, 0))
        a2 = jnp.where(row == 0, c[SUBLANES - 2:SUBLANES - 1],
                       jnp.where(row == 1, c[SUBLANES - 1:SUBLANES], pltpu.roll(a, 2, 0)))
        carry_s[f] = a[tm - SUBLANES:, :]
        nb_ref[0] = a[tm - (FFN_CONV_W - 1):, :]
    cw = cw_ref[...]
    ac = cb_ref[...] + a2 * cw[0:1] + a1 * cw[1:2] + a * cw[2:3]
    act = (_gelu(ac) * gate).astype(BF16)
    acc_s[...] += _dot(act, wd_ref[...])

    @pl.when(f == pl.num_programs(1) - 1)
    def _():
        out = x1_s[...] + acc_s[...]
        if final_norm:
            out = _rms(out, fg_ref[...])
        o_ref[...] = out


def _ffn(x2d, mix, wo, g, wa, wb, cw, cb, wd, b, t, tm, tf, prev=None, final_g=None):
    m = x2d.shape[0]
    kmix = mix.shape[1]
    paged_prev = prev is not None
    tiles_per_seq = max(t // tm, 1)
    rows = lambda n: pl.BlockSpec((tm, n), lambda i, f: (i, 0))
    cols = lambda r: pl.BlockSpec((r, tf), lambda i, f: (0, f))
    in_specs = [rows(D_MODEL), rows(kmix), _full((kmix, D_MODEL)), _full((1, D_MODEL)),
                cols(D_MODEL), cols(D_MODEL), cols(FFN_CONV_W), cols(1),
                pl.BlockSpec((tf, D_MODEL), lambda i, f: (f, 0))]
    args = [x2d, mix, wo, g, wa, wb, cw, cb, wd]
    scratch = [pltpu.VMEM((tm, D_MODEL), F32), pltpu.VMEM((tm, D_MODEL), BF16), pltpu.VMEM((tm, D_MODEL), F32)]
    if paged_prev:
        in_specs += [pl.BlockSpec((tm, tf), lambda i, f: (i, f))] * 2
        args += list(prev)
        nb_spec = pl.BlockSpec((tm, tf), lambda i, f: (i, f))
        nb_shape = jax.ShapeDtypeStruct((m, D_FF), F32)
    else:
        nb_spec = pl.BlockSpec((1, FFN_CONV_W - 1, tf), lambda i, f: (i, 0, f))
        nb_shape = jax.ShapeDtypeStruct((m // tm, FFN_CONV_W - 1, D_FF), F32)
        scratch.append(pltpu.VMEM((D_FF // tf, SUBLANES, tf), F32))
    if final_g is not None:
        in_specs.append(_full((1, D_MODEL)))
        args.append(final_g)
    return pl.pallas_call(
        functools.partial(_ffn_kernel, tm=tm, tf=tf, tiles_per_seq=tiles_per_seq,
                          paged_prev=paged_prev, final_norm=final_g is not None),
        grid=(m // tm, D_FF // tf),
        in_specs=in_specs,
        out_specs=(rows(D_MODEL), nb_spec),
        out_shape=(jax.ShapeDtypeStruct((m, D_MODEL), F32), nb_shape),
        scratch_shapes=scratch,
        compiler_params=_params(2),
        name="ffn",
    )(*args)


def _lru_kernel(*refs, tm, tiles_per_seq, paged_prev):
    it = iter(refs)
    x_ref, g_ref, wg_ref, wx_ref, cw_ref, cb_ref, wri_ref, bri_ref, lam_ref = (next(it) for _ in range(9))
    if paged_prev:
        p1_ref, p2_ref, p3_ref, h0_ref = (next(it) for _ in range(4))
    mix_ref, u_ref, hl_ref = next(it), next(it), next(it)
    a_s, bx_s, hs_s = next(it), next(it), next(it)
    if not paged_prev:
        cu_s, ch_s = next(it), next(it)
    mi = pl.program_id(0)

    h = _rms(x_ref[...], g_ref[...]).astype(BF16)
    u = _dot(h, wx_ref[...])
    row = lax.broadcasted_iota(jnp.int32, (tm, D_RNN), 0)
    if paged_prev:
        t = row & (SUBLANES - 1)
        u1 = jnp.where(t == 0, p1_ref[...], pltpu.roll(u, 1, 0))
        u2 = jnp.where(t < 2, p2_ref[...], pltpu.roll(u, 2, 0))
        u3 = jnp.where(t < 3, p3_ref[...], pltpu.roll(u, 3, 0))
        u_ref[...] = u
    else:
        @pl.when((mi % tiles_per_seq) == 0)
        def _():
            cu_s[...] = jnp.zeros(cu_s.shape, F32)
            ch_s[...] = jnp.zeros(ch_s.shape, F32)

        c = cu_s[...]
        l1, l2, l3 = (c[SUBLANES - k:SUBLANES - k + 1] for k in (1, 2, 3))
        u1 = jnp.where(row == 0, l1, pltpu.roll(u, 1, 0))
        u2 = jnp.where(row == 0, l2, jnp.where(row == 1, l1, pltpu.roll(u, 2, 0)))
        u3 = jnp.where(row == 0, l3, jnp.where(row == 1, l2, jnp.where(row == 2, l1, pltpu.roll(u, 3, 0))))
        cu_s[...] = u[tm - SUBLANES:, :]
        u_ref[0] = u[tm - (LRU_CONV_W - 1):, :]
    cw = cw_ref[...]
    xc = cb_ref[...] + u3 * cw[0:1] + u2 * cw[1:2] + u1 * cw[2:3] + u * cw[3:4]
    ri = _dot(xc.astype(BF16), wri_ref[...]) + bri_ref[...]
    rg = _sigmoid(ri[:, :D_RNN])
    ig = _sigmoid(ri[:, D_RNN:])
    log_a = LRU_C * rg * _log_sigmoid(lam_ref[...])
    av = jnp.exp(log_a)
    a_s[...] = av
    bx_s[...] = jnp.sqrt(-jnp.tanh(log_a) * (av * av + 1.0)) * (ig * xc)

    r8 = lax.broadcasted_iota(jnp.int32, (SUBLANES, D_RNN), 0)

    def group(gi, hc):
        sl = pl.ds(pl.multiple_of(gi * SUBLANES, SUBLANES), SUBLANES)
        av = a_s[sl, :]
        bv = bx_s[sl, :]
        for s in (1, 2, 4):
            keep = r8 >= s
            bv = jnp.where(keep, av * pltpu.roll(bv, s, 0) + bv, bv)
            av = jnp.where(keep, av * pltpu.roll(av, s, 0), av)
        if paged_prev:
            hc = h0_ref[pl.ds(gi, 1), :]
        hs = av * hc + bv
        hs_s[sl, :] = hs
        return hs[SUBLANES - 1:SUBLANES, :]

    if paged_prev:
        hc0 = jnp.zeros((1, D_RNN), F32)
    else:
        hc0 = ch_s[SUBLANES - 1:SUBLANES, :]
    lax.fori_loop(0, tm // SUBLANES, group, hc0)
    hs = hs_s[...]
    if paged_prev:
        hl_ref[...] = hs
    else:
        ch_s[...] = hs[tm - SUBLANES:, :]
        hl_ref[0] = hs[tm - 1:, :]
    gate = _gelu(_dot(h, wg_ref[...]))
    mix_ref[...] = (hs * gate).astype(BF16)


def _lru(x2d, g, wg, wx, cw, cb, wri, bri, lam, b, t, tm, prev=None):
    m = x2d.shape[0]
    paged_prev = prev is not None
    tiles_per_seq = max(t // tm, 1)
    rows = lambda n: pl.BlockSpec((tm, n), lambda i: (i, 0))
    in_specs = [rows(D_MODEL), _full((1, D_MODEL)), _full((D_MODEL, D_RNN)), _full((D_MODEL, D_RNN)),
                _full((LRU_CONV_W, D_RNN)), _full((1, D_RNN)), _full((D_RNN, 2 * D_RNN)), _full((1, 2 * D_RNN)),
                _full((1, D_RNN))]
    args = [x2d, g, wg, wx, cw, cb, wri, bri, lam]
    scratch = [pltpu.VMEM((tm, D_RNN), F32)] * 3
    if paged_prev:
        in_specs += [rows(D_RNN)] * 3 + [_full(prev[3].shape)]
        args += list(prev)
        u_spec, hl_spec = rows(D_RNN), rows(D_RNN)
        u_shape = hl_shape = jax.ShapeDtypeStruct((m, D_RNN), F32)
    else:
        u_spec = pl.BlockSpec((1, LRU_CONV_W - 1, D_RNN), lambda i: (i, 0, 0))
        hl_spec = pl.BlockSpec((1, 1, D_RNN), lambda i: (i, 0, 0))
        u_shape = jax.ShapeDtypeStruct((m // tm, LRU_CONV_W - 1, D_RNN), F32)
        hl_shape = jax.ShapeDtypeStruct((m // tm, 1, D_RNN), F32)
        scratch += [pltpu.VMEM((SUBLANES, D_RNN), F32)] * 2
    return pl.pallas_call(
        functools.partial(_lru_kernel, tm=tm, tiles_per_seq=tiles_per_seq, paged_prev=paged_prev),
        grid=(m // tm,),
        in_specs=in_specs,
        out_specs=(rows(D_RNN), u_spec, hl_spec),
        out_shape=(jax.ShapeDtypeStruct((m, D_RNN), BF16), u_shape, hl_shape),
        scratch_shapes=scratch,
        compiler_params=_params(1),
        name="lru",
    )(*args)


def _rope_tables(pos):
    half = ROT_DIMS // 2
    inv = ROPE_THETA ** (-jnp.arange(half, dtype=F32) / half)
    ang = pos.astype(F32)[:, None] * inv[None, :]
    lane = jnp.arange(LANES) % DH_DIFF
    cos = jnp.cos(ang)[:, lane % half]
    sin = jnp.sin(ang)[:, lane % half]
    cos_t = jnp.where(lane[None, :] < ROT_DIMS, cos, 1.0)
    sin_t = jnp.where(lane[None, :] < half, -sin, jnp.where(lane[None, :] < ROT_DIMS, sin, 0.0))
    return cos_t, sin_t


def _block_diag(w):
    n, c, d = w.shape
    idx = jnp.arange(n)
    return jnp.zeros((n, c, n, d), w.dtype).at[idx, :, idx, :].set(w).reshape(n * c, n * d)


def _expand_prev(buf, t, k):
    bs, wm1, c = buf.shape
    out = jnp.zeros((bs, t, c), buf.dtype)
    for step in range(min(k, t)):
        out = out.at[:, step].set(buf[:, wm1 - k + step])
    return out.reshape(bs * t, c)


def kernel(x_prompt, x_sample, cache_fox_k, cache_fox_v, cache_fox_logf, cache_diff_k, cache_diff_v, state_lru_conv, state_lru_h, state_ffn_conv, page_table, mix_norm_g, ab_w_in, ab_b_f, ab_lam_q1, ab_lam_k1, ab_lam_q2, ab_lam_k2, ab_subln_g, ab_w_out, lru_w_gate, lru_w_x, lru_conv_w, lru_conv_b, lru_w_a, lru_b_a, lru_w_i, lru_b_i, lru_lambda, lru_w_out, ffn_norm_g, ffn_w_a, ffn_w_b, ffn_conv_w, ffn_conv_b, ffn_w_down, final_norm_g):
    depth = mix_norm_g.shape[0]
    assert depth == 2 and ab_w_in.shape[0] == 1 and lru_w_x.shape[0] == 1

    w_in = ab_w_in[0]
    o_f = 3 * D_FOX
    w_all = jnp.concatenate([w_in[:, :o_f], w_in[:, o_f + H_FOX:], w_in[:, o_f:o_f + H_FOX],
                             jnp.zeros((D_MODEL, LANES - H_FOX), F32)], axis=1).astype(BF16)
    bf_pad = jnp.pad(ab_b_f[0], (0, LANES - H_FOX)).reshape(1, LANES)
    lamv = jnp.stack([ab_lam_q1[0], ab_lam_k1[0], ab_lam_q2[0], ab_lam_k2[0]]).astype(F32)
    subln_g = ab_subln_g[0].reshape(1, LANES)
    w_out = ab_w_out[0].astype(BF16)
    wg = lru_w_gate[0].astype(BF16)
    wx = lru_w_x[0].astype(BF16)
    wri = jnp.concatenate([_block_diag(lru_w_a[0]), _block_diag(lru_w_i[0])], axis=1).astype(BF16)
    bri = jnp.concatenate([lru_b_a[0], lru_b_i[0]]).reshape(1, 2 * D_RNN)
    lru_wo = lru_w_out[0].astype(BF16)
    ffn_wa = ffn_w_a.astype(BF16)
    ffn_wb = ffn_w_b.astype(BF16)
    ffn_wd = ffn_w_down.astype(BF16)
    lam_init = 0.8 - 0.6 * math.exp(-0.3 * 0)

    def layer0_proj(x2d, pos_rows, tm):
        cos_t, sin_t = _rope_tables(pos_rows)
        return _inproj(x2d, mix_norm_g[0].reshape(1, D_MODEL), w_all, bf_pad, cos_t, sin_t, tm)

    def tail(x2d, mix0, b, t, tm_ffn, tm_lru, tf, ffn_prev, lru_prev):
        x1, nb0 = _ffn(x2d, mix0, w_out, ffn_norm_g[0].reshape(1, D_MODEL), ffn_wa[0], ffn_wb[0],
                       ffn_conv_w[0], ffn_conv_b[0].reshape(1, D_FF), ffn_wd[0], b, t, tm_ffn, tf,
                       prev=ffn_prev[0])
        mix1, u_out, h_out = _lru(x1, mix_norm_g[1].reshape(1, D_MODEL), wg, wx, lru_conv_w[0],
                                  lru_conv_b[0].reshape(1, D_RNN), wri, bri, lru_lambda[0].reshape(1, D_RNN),
                                  b, t, tm_lru, prev=lru_prev)
        y, nb1 = _ffn(x1, mix1, lru_wo, ffn_norm_g[1].reshape(1, D_MODEL), ffn_wa[1], ffn_wb[1],
                      ffn_conv_w[1], ffn_conv_b[1].reshape(1, D_FF), ffn_wd[1], b, t, tm_ffn, tf,
                      prev=ffn_prev[1], final_g=final_norm_g.reshape(1, D_MODEL))
        return y, nb0, nb1, u_out, h_out

    bp, tp, _ = x_prompt.shape
    mp = bp * tp
    xp = x_prompt.reshape(mp, D_MODEL)
    (qf, kf, vf, kfb, vfb, logf, logft, qd, kd, vd, kdb, vdb) = layer0_proj(xp, jnp.arange(tp), 512)
    cum = _cumsum_prompt(logft, bp, tp)
    r3 = lambda a: a.reshape(bp, tp, D_FOX)
    tq = 512
    of = _attn_prompt(r3(qf), r3(kfb), r3(vfb), (cum,), True, tq, lam_init)
    od = _attn_prompt(r3(qd), r3(kdb), r3(vdb), (lamv, subln_g), False, tq, lam_init)
    mix0 = jnp.concatenate([of, od], axis=-1).reshape(mp, D_MIX_AB)
    tm_ffn, tm_lru = 512, 256
    y_p, nb0, nb1, u_out, h_out = tail(xp, mix0, bp, tp, tm_ffn, tm_lru, 512, (None, None), None)
    seq_last = lambda a, tm: a[tp // tm - 1::tp // tm]
    nb0, nb1, u_out, h_out = seq_last(nb0, tm_ffn), seq_last(nb1, tm_ffn), seq_last(u_out, tm_lru), seq_last(h_out, tm_lru)
    y_prompt = y_p.reshape(bp, tp, D_MODEL)
    p_state = (kf.reshape(1, bp, tp, H_FOX, DH_FOX), vf.reshape(1, bp, tp, H_FOX, DH_FOX),
               logf.reshape(1, bp, tp, H_FOX),
               kd.reshape(1, bp, tp, H_DIFF, 2 * DH_DIFF), vd.reshape(1, bp, tp, H_DIFF, 2 * DH_DIFF),
               u_out[None], h_out.reshape(1, bp, D_RNN), jnp.stack([nb0, nb1]))

    bs, ts, _ = x_sample.shape
    assert ts == SUBLANES
    ms = bs * ts
    n_pages = page_table.shape[1]
    past = n_pages * PAGE_SIZE
    xs = x_sample.reshape(ms, D_MODEL)
    pos_rows = jnp.tile(past + jnp.arange(ts), bs)
    (qf, kf, vf, kfb, vfb, logf, logft, qd, kd, vd, kdb, vdb) = layer0_proj(xs, pos_rows, ms)
    n_pool = cache_fox_k.shape[1]
    logf_pool = jnp.swapaxes(cache_fox_logf[0], 1, 2).reshape(n_pool * H_FOX, PAGE_SIZE)
    lc = _cumsum_rows(logf_pool).reshape(n_pool, H_FOX, PAGE_SIZE)
    logfn = jnp.pad(logft.reshape(H_FOX, bs, ts).transpose(1, 0, 2), ((0, 0), (0, 0), (0, LANES - ts)))
    pad_rows = lambda a: jnp.pad(a.reshape(bs, ts, D_FOX), ((0, 0), (0, PAGE_SIZE - ts), (0, 0)))
    kt_pages = lambda c: jnp.transpose(c, (0, 2, 3, 1)).reshape(n_pool, D_FOX, PAGE_SIZE)
    mix0 = _attn_decode(page_table, qf.astype(F32).reshape(bs, ts, D_FOX), qd.astype(F32).reshape(bs, ts, D_DIFF),
                        pad_rows(kfb), pad_rows(vfb), pad_rows(kdb), pad_rows(vdb),
                        kt_pages(cache_fox_k[0]), kt_pages(cache_fox_v[0]), lc,
                        cache_diff_k[0].reshape(n_pool, PAGE_SIZE * H_DIFF, 2 * DH_DIFF),
                        cache_diff_v[0].reshape(n_pool, PAGE_SIZE * H_DIFF, 2 * DH_DIFF),
                        logfn, lamv, subln_g, 8, lam_init).reshape(ms, D_MIX_AB)
    ffn_prev = tuple((_expand_prev(state_ffn_conv[l], ts, 1), _expand_prev(state_ffn_conv[l], ts, 2))
                     for l in range(depth))
    lru_prev = tuple(_expand_prev(state_lru_conv[0], ts, k) for k in (1, 2, 3)) + (state_lru_h[0],)
    y_s, a0, a1, u_full, h_full = tail(xs, mix0, bs, ts, ms, ms, 512, ffn_prev, lru_prev)
    y_sample = y_s.reshape(bs, ts, D_MODEL)
    last = lambda a, k: a.reshape(bs, ts, -1)[:, ts - k:]
    s_state = (kf.reshape(1, bs, ts, H_FOX, DH_FOX), vf.reshape(1, bs, ts, H_FOX, DH_FOX),
               logf.reshape(1, bs, ts, H_FOX),
               kd.reshape(1, bs, ts, H_DIFF, 2 * DH_DIFF), vd.reshape(1, bs, ts, H_DIFF, 2 * DH_DIFF),
               last(u_full, LRU_CONV_W - 1)[None], last(h_full, 1).reshape(1, bs, D_RNN),
               jnp.stack([last(a0, FFN_CONV_W - 1), last(a1, FFN_CONV_W - 1)]))

    return (y_prompt, y_sample) + p_state + s_state
```

```python
import functools
import math

import jax
import jax.numpy as jnp
from jax import lax
from jax.experimental import pallas as pl
from jax.experimental.pallas import tpu as pltpu

F32 = jnp.float32
BF16 = jnp.bfloat16

D_MODEL = 1024
H_FOX = 8
DH_FOX = 64
H_DIFF = 4
DH_DIFF = 64
D_FOX = H_FOX * DH_FOX
D_DIFF = H_DIFF * 2 * DH_DIFF
D_MIX_AB = D_FOX + D_DIFF
ROT_DIMS = DH_DIFF // 4
ROPE_THETA = 500000.0
D_RNN = 1280
N_LRU_BLOCKS = 16
LRU_BW = D_RNN // N_LRU_BLOCKS
LRU_CONV_W = 4
LRU_C = 8.0
D_FF = 3072
FFN_CONV_W = 3
EPS = 1e-6
NEG = -1e30
PAGE_SIZE = 128
LOG2E = math.log2(math.e)

LANES = 128
SUBLANES = 8
VMEM_LIMIT = 56 * 1024 * 1024
N_QKV = 6 * D_FOX
N_INPROJ = N_QKV + LANES


def _params(n_axes):
    return pltpu.CompilerParams(dimension_semantics=("arbitrary",) * n_axes,
                                vmem_limit_bytes=VMEM_LIMIT)


def _full(shape):
    n = len(shape)
    return pl.BlockSpec(shape, lambda *_: (0,) * n)


def _resident(shape):
    n = len(shape)
    return pl.BlockSpec(shape, lambda *_: (0,) * n, pipeline_mode=pl.Buffered(1))


def _rms(x, g):
    return x * lax.rsqrt(jnp.mean(x * x, axis=-1, keepdims=True) + EPS) * g


def _gelu(x):
    c = math.sqrt(2.0 / math.pi)
    return x * (0.5 * (1.0 + jnp.tanh(c * (x + 0.044715 * (x * x * x)))))


def _log_sigmoid(z):
    return -(jnp.maximum(-z, 0.0) + jnp.log1p(jnp.exp(-jnp.abs(z))))


def _sigmoid(z):
    return 1.0 / (1.0 + jnp.exp(-z))


def _dot(a, b):
    return jnp.dot(a, b, preferred_element_type=F32)


def _dot_nt(a, b):
    return lax.dot_general(a, b, (((1,), (1,)), ((), ())), preferred_element_type=F32)


def _prefix_lanes(x):
    r = lax.broadcasted_iota(jnp.int32, (LANES, LANES), 0)
    c = lax.broadcasted_iota(jnp.int32, (LANES, LANES), 1)
    tri = jnp.where(r <= c, 1.0, 0.0).astype(BF16)
    hi = x.astype(BF16)
    r1 = x - hi.astype(F32)
    mid = r1.astype(BF16)
    lo = (r1 - mid.astype(F32)).astype(BF16)
    return _dot(hi, tri) + _dot(mid, tri) + _dot(lo, tri)


def _inproj_kernel(x_ref, g_ref, w_ref, bf_ref, cos_ref, sin_ref,
                   qf_ref, kf_ref, vf_ref, kfb_ref, vfb_ref, logf_ref, logft_ref,
                   qd_ref, kd_ref, vd_ref, kdb_ref, vdb_ref):
    h = _rms(x_ref[...], g_ref[...]).astype(BF16)

    def proj(i, width=D_FOX):
        return _dot(h, w_ref[:, i * D_FOX:i * D_FOX + width])

    qf_ref[...] = (proj(0) * (DH_FOX ** -0.5 * LOG2E)).astype(BF16)
    kf = proj(1)
    kf_ref[...] = kf
    kfb_ref[...] = kf.astype(BF16)
    vf = proj(2)
    vf_ref[...] = vf
    vfb_ref[...] = vf.astype(BF16)

    logf = _log_sigmoid(proj(6, LANES) + bf_ref[...])
    logf_ref[...] = logf[:, :H_FOX]
    logft_ref[...] = logf.T[:H_FOX, :]

    cosf = jnp.concatenate([cos_ref[...]] * (D_DIFF // LANES), axis=1)
    sinf = jnp.concatenate([sin_ref[...]] * (D_DIFF // LANES), axis=1)
    lane = lax.broadcasted_iota(jnp.int32, cosf.shape, 1) & (DH_DIFF - 1)
    half = ROT_DIMS // 2

    def rope(x):
        partner = jnp.where(lane < half, pltpu.roll(x, D_DIFF - half, 1), pltpu.roll(x, half, 1))
        return jnp.where(lane < ROT_DIMS, x * cosf + partner * sinf, x)

    qd_ref[...] = (rope(proj(3)) * (DH_DIFF ** -0.5 * LOG2E)).astype(BF16)
    kd = rope(proj(4))
    kd_ref[...] = kd
    kdb_ref[...] = kd.astype(BF16)
    vd = proj(5)
    vd_ref[...] = vd
    vdb_ref[...] = vd.astype(BF16)


def _inproj(x2d, g, w_all, bf_pad, cos_t, sin_t, tm):
    m = x2d.shape[0]
    nt = cos_t.shape[0] // tm
    row = lambda n: pl.BlockSpec((tm, n), lambda i: (i, 0))
    tab = pl.BlockSpec((tm, LANES), lambda i: (i % nt, 0))
    f32o = jax.ShapeDtypeStruct((m, D_FOX), F32)
    b16o = jax.ShapeDtypeStruct((m, D_FOX), BF16)
    out_shape = (b16o, f32o, f32o, b16o, b16o,
                 jax.ShapeDtypeStruct((m, H_FOX), F32), jax.ShapeDtypeStruct((H_FOX, m), F32),
                 b16o, f32o, f32o, b16o, b16o)
    out_specs = (row(D_FOX),) * 5 + (row(H_FOX), pl.BlockSpec((H_FOX, tm), lambda i: (0, i))) + (row(D_FOX),) * 5
    return pl.pallas_call(
        _inproj_kernel,
        grid=(m // tm,),
        in_specs=[row(D_MODEL), _full((1, D_MODEL)), _full((D_MODEL, N_INPROJ)), _full((1, LANES)), tab, tab],
        out_specs=out_specs,
        out_shape=out_shape,
        compiler_params=_params(1),
        name="inproj",
    )(x2d, g, w_all, bf_pad, cos_t, sin_t)


def _cumsum_prompt_kernel(x_ref, o_ref, *, t):
    carry = jnp.zeros((H_FOX, 1), F32)
    for i in range(t // LANES):
        c = _prefix_lanes(x_ref[:, i * LANES:(i + 1) * LANES]) + carry
        c2 = c * LOG2E
        for j in range(H_FOX // 2):
            o_ref[0, j, :, i * LANES:(i + 1) * LANES] = c2[2 * j:2 * j + 2, :]
        carry = c[:, LANES - 1:LANES]


def _cumsum_prompt(logft, b, t):
    return pl.pallas_call(
        functools.partial(_cumsum_prompt_kernel, t=t),
        grid=(b,),
        in_specs=[pl.BlockSpec((H_FOX, t), lambda i: (0, i))],
        out_specs=pl.BlockSpec((1, H_FOX // 2, 2, t), lambda i: (i, 0, 0, 0)),
        out_shape=jax.ShapeDtypeStruct((b, H_FOX // 2, 2, t), F32),
        compiler_params=_params(1),
        name="cumsum_prompt",
    )(logft)


def _cumsum_rows_kernel(x_ref, o_ref):
    o_ref[...] = _prefix_lanes(x_ref[...])


def _cumsum_rows(x, max_rows=2048):
    rows = x.shape[0]
    tr = max(d for d in range(SUBLANES, max_rows + 1, SUBLANES) if rows % d == 0)
    spec = pl.BlockSpec((tr, LANES), lambda i: (i, 0))
    return pl.pallas_call(
        _cumsum_rows_kernel,
        grid=(rows // tr,),
        in_specs=[spec],
        out_specs=spec,
        out_shape=jax.ShapeDtypeStruct((rows, LANES), F32),
        compiler_params=_params(1),
        name="cumsum_pages",
    )(x)


def _stack_masked(q):
    lane = lax.broadcasted_iota(jnp.int32, q.shape, 1)
    zero = jnp.zeros_like(q)
    return jnp.concatenate([jnp.where(lane < LANES // 2, q, zero),
                            jnp.where(lane >= LANES // 2, q, zero)], axis=0)


def _lambda(lamv, lam_init):
    s1 = jnp.sum(lamv[0:1] * lamv[1:2], axis=1, keepdims=True)
    s2 = jnp.sum(lamv[2:3] * lamv[3:4], axis=1, keepdims=True)
    return jnp.exp(s1) - jnp.exp(s2) + lam_init


def _fox_out(acc, l, r):
    o = acc / l
    lane = lax.broadcasted_iota(jnp.int32, (r, LANES), 1)
    return jnp.where(lane < LANES // 2, o[:r], o[r:])


def _diff_out(acc, l, r, lamv, g, lam_init):
    o = acc / l
    od = o[:r] - _lambda(lamv, lam_init) * o[r:]
    return _rms(od, g) * (1.0 - lam_init)


def _attn_prompt_kernel(*refs, is_fox, tq, rc, lam_init):
    if is_fox:
        q_ref, k_ref, v_ref, c_ref, o_ref = refs[:5]
    else:
        q_ref, k_ref, v_ref, lamv_ref, g_ref, o_ref = refs[:6]
    s0_s, s1_s, r0_s, r1_s, m_s, l_s, acc_s = refs[-7:]
    qi = pl.program_id(2)
    q2 = _stack_masked(q_ref[0])
    nrep = tq // LANES

    def keys(kb):
        return pl.ds(pl.multiple_of(kb * tq, tq), tq)

    def row_max(s):
        return jnp.broadcast_to(jnp.max(s, axis=1, keepdims=True), (2 * tq, LANES))

    def qk(kb, s_s, r_s):
        s = _dot_nt(q2, k_ref[0, keys(kb), :])
        if is_fox:
            c0 = c_ref[0, 0, 0:1, keys(kb)]
            c1 = c_ref[0, 0, 1:2, keys(kb)]
            s = jnp.concatenate([s[:tq] - c0, s[tq:] - c1], axis=0)
        s_s[...] = s
        r_s[...] = row_max(s)

    def softmax_pv(kb, s_s, r_s):
        m_old = m_s[...]
        m_new = jnp.maximum(m_old, r_s[...])
        alpha = jnp.exp2(m_old - m_new)
        m_s[...] = m_new
        sums, probs = [], []
        for ci in range(2 * tq // rc):
            rows = slice(ci * rc, (ci + 1) * rc)
            p = jnp.exp2(s_s[rows, :] - jnp.tile(m_new[rows], (1, nrep)))
            sums.append(sum(p[:, i * LANES:(i + 1) * LANES] for i in range(nrep)))
            probs.append(p.astype(BF16))
        l_s[...] = alpha * l_s[...] + jnp.concatenate(sums, axis=0)
        acc_s[...] = alpha * acc_s[...] + _dot(jnp.concatenate(probs, axis=0), v_ref[0, keys(kb), :])

    m_s[...] = jnp.full(m_s.shape, NEG, F32)
    l_s[...] = jnp.zeros(l_s.shape, F32)
    acc_s[...] = jnp.zeros(acc_s.shape, F32)
    bufs = ((s0_s, r0_s), (s1_s, r1_s))
    qk(0, *bufs[0])

    def body(kb, _):
        for parity in range(2):
            @pl.when((kb & 1) == parity)
            def _():
                qk(kb + 1, *bufs[1 - parity])
                softmax_pv(kb, *bufs[parity])
        return 0

    lax.fori_loop(0, qi, body, 0)
    row = lax.broadcasted_iota(jnp.int32, (2 * tq, tq), 0)
    col = lax.broadcasted_iota(jnp.int32, (2 * tq, tq), 1)
    causal = col <= jnp.where(row >= tq, row - tq, row)
    for parity in range(2):
        @pl.when((qi & 1) == parity)
        def _():
            s_s, r_s = bufs[parity]
            s = jnp.where(causal, s_s[...], NEG)
            s_s[...] = s
            r_s[...] = row_max(s)
            softmax_pv(qi, s_s, r_s)

    acc = acc_s[...]
    l = jnp.sum(l_s[...], axis=1, keepdims=True)
    if is_fox:
        o_ref[0] = _fox_out(acc, l, tq).astype(BF16)
    else:
        o_ref[0] = _diff_out(acc, l, tq, lamv_ref[...], g_ref[...], lam_init).astype(BF16)


def _attn_prompt(q, k, v, extra, is_fox, tq, lam_init, rc=32):
    b, t, _ = q.shape
    nblk = D_FOX // LANES
    wide = pltpu.VMEM((2 * tq, LANES), F32)
    scratch = [pltpu.VMEM((2 * tq, tq), F32), pltpu.VMEM((2 * tq, tq), F32), wide, wide, wide, wide, wide]
    qspec = pl.BlockSpec((1, tq, LANES), lambda bi, j, qi: (bi, qi, j))
    kvspec = pl.BlockSpec((1, t, LANES), lambda bi, j, qi: (bi, 0, j))
    if is_fox:
        especs = [pl.BlockSpec((1, 1, 2, t), lambda bi, j, qi: (bi, j, 0, 0))]
    else:
        especs = [_full((4, DH_DIFF)), _full((1, LANES))]
    return pl.pallas_call(
        functools.partial(_attn_prompt_kernel, is_fox=is_fox, tq=tq, rc=rc, lam_init=lam_init),
        grid=(b, nblk, t // tq),
        in_specs=[qspec, kvspec, kvspec] + especs,
        out_specs=qspec,
        out_shape=jax.ShapeDtypeStruct((b, t, D_FOX), BF16),
        scratch_shapes=scratch,
        compiler_params=_params(3),
        name="attn_prompt_fox" if is_fox else "attn_prompt_diff",
    )(q, k, v, *extra)


def _attn_decode_kernel(pt_ref, *refs, g_pages, n_steps, t_new, lam_init):
    n_in = 6 + 5 * g_pages + 5
    (qf_ref, qd_ref, kfn_ref, vfn_ref, kdn_ref, vdn_ref) = refs[:6]
    pages = refs[6:6 + 5 * g_pages]
    fk_refs, fv_refs, lc_refs, dk_refs, dv_refs = (pages[i * g_pages:(i + 1) * g_pages] for i in range(5))
    logfn_ref, lamv_ref, g_ref = refs[6 + 5 * g_pages:6 + 5 * g_pages + 3]
    o_ref = refs[n_in - 2]
    q2_s, m_s, l_s, acc_s, cc_s = refs[n_in - 1:]
    del pt_ref
    p = pl.program_id(1)
    nfb = D_FOX // LANES
    ndb = D_DIFF // LANES
    r = SUBLANES

    nblk = nfb + ndb
    rb = 2 * r

    def blk(x, j):
        return x[j * rb:(j + 1) * rb]

    def lanes(x, j):
        return x[:, j * LANES:(j + 1) * LANES]

    @pl.when(p == 0)
    def _():
        q = jnp.concatenate([qf_ref[0], qd_ref[0]], axis=1)
        q2_s[...] = jnp.concatenate([_stack_masked(lanes(q, j)) for j in range(nblk)], axis=0).astype(BF16)
        m_s[...] = jnp.full(m_s.shape, NEG, F32)
        l_s[...] = jnp.zeros(l_s.shape, F32)
        acc_s[...] = jnp.zeros(acc_s.shape, F32)
        cc_s[...] = jnp.zeros(cc_s.shape, F32)

    def fox_bias(ck):
        ck = ck * LOG2E
        return jnp.concatenate([jnp.broadcast_to(ck[h:h + 1], (r, ck.shape[1])) for h in range(H_FOX)], axis=0)

    nf = nfb * rb
    nd = ndb * rb

    def update(lo, n, u, pv_fn):
        sl = slice(lo, lo + n)
        m = m_s[sl]
        m_new = jnp.maximum(m, jnp.max(u, axis=1, keepdims=True))
        alpha = jnp.exp2(m - m_new)
        pe = jnp.exp2(u - m_new)
        m_s[sl] = m_new
        l_s[sl] = alpha * l_s[sl] + jnp.sum(pe, axis=1, keepdims=True)
        acc_s[sl] = alpha * acc_s[sl] + pv_fn(pe.astype(BF16))

    @pl.when(p < n_steps)
    def _():
        cks = []
        run = cc_s[...]
        for gi in range(g_pages):
            lc = lc_refs[gi][0]
            cks.append(run + lc)
            run = run + lc[:, LANES - 1:LANES]
        cc_s[...] = run
        q2 = q2_s[...]

        def fkv(refs, gi, j):
            return refs[gi][0, j * LANES:(j + 1) * LANES, :].astype(BF16)

        s_fox = jnp.concatenate(
            [jnp.concatenate([_dot(blk(q2, j), fkv(fk_refs, gi, j)) for gi in range(g_pages)], axis=1)
             for j in range(nfb)], axis=0)

        def pv_fox(pb):
            return jnp.concatenate(
                [sum(_dot_nt(lanes(blk(pb, j), gi), fkv(fv_refs, gi, j)) for gi in range(g_pages))
                 for j in range(nfb)], axis=0)

        update(0, nf, s_fox - fox_bias(jnp.concatenate(cks, axis=1)), pv_fox)

        wd = PAGE_SIZE * H_DIFF
        s_diff = jnp.concatenate([_dot_nt(q2[nf:], dk_refs[gi][0].astype(BF16)) for gi in range(g_pages)], axis=1)
        row = lax.broadcasted_iota(jnp.int32, s_diff.shape, 0)
        col = lax.broadcasted_iota(jnp.int32, s_diff.shape, 1)
        own = (col & (H_DIFF - 1)) == row // rb

        def pv_diff(pb):
            return sum(_dot(pb[:, gi * wd:(gi + 1) * wd], dv_refs[gi][0].astype(BF16)) for gi in range(g_pages))

        update(nf, nd, jnp.where(own, s_diff, NEG), pv_diff)

    @pl.when(p == n_steps)
    def _():
        row = lax.broadcasted_iota(jnp.int32, (nf, LANES), 0) & (r - 1)
        col = lax.broadcasted_iota(jnp.int32, (nf, LANES), 1)
        keep = (col <= row) & (col < t_new)
        ck = cc_s[...] + _prefix_lanes(logfn_ref[0])
        q2 = q2_s[...]
        s_fox = jnp.concatenate([_dot_nt(blk(q2, j), lanes(kfn_ref[0], j)) for j in range(nfb)], axis=0)
        update(0, nf, jnp.where(keep, s_fox - fox_bias(ck), NEG),
               lambda pb: jnp.concatenate([_dot(blk(pb, j), lanes(vfn_ref[0], j)) for j in range(nfb)], axis=0))
        s_diff = jnp.concatenate([_dot_nt(blk(q2, nfb + j), lanes(kdn_ref[0], j)) for j in range(ndb)], axis=0)
        update(nf, nd, jnp.where(keep, s_diff, NEG),
               lambda pb: jnp.concatenate([_dot(blk(pb, j), lanes(vdn_ref[0], j)) for j in range(ndb)], axis=0))
        acc = acc_s[...]
        l = l_s[...]
        lamv = lamv_ref[...]
        for j in range(nfb):
            o_ref[0, :, j * LANES:(j + 1) * LANES] = _fox_out(blk(acc, j), blk(l, j), r).astype(BF16)
        for j in range(nfb, nblk):
            o = _diff_out(blk(acc, j), blk(l, j), r, lamv, g_ref[...], lam_init)
            o_ref[0, :, j * LANES:(j + 1) * LANES] = o.astype(BF16)


def _attn_decode(page_table, qf, qd, kfn, vfn, kdn, vdn, fk, fv, lc, dk, dv, logfn, lamv, g, g_pages, lam_init):
    bs, t_new, _ = qf.shape
    n_pages = page_table.shape[1]
    n_steps = n_pages // g_pages
    nblk = (D_FOX + D_DIFF) // LANES

    def seq(shape):
        return pl.BlockSpec((1,) + shape, lambda b, p, pt: (b, 0, 0))

    def paged(shape, gi):
        def index(b, p, pt):
            return (pt[b, jnp.minimum(p, n_steps - 1) * g_pages + gi],) + (0,) * len(shape)
        return pl.BlockSpec((1,) + shape, index)

    in_specs = [seq((t_new, D_FOX)), seq((t_new, D_DIFF))] + [seq((PAGE_SIZE, D_FOX))] * 4
    args = [qf, qd, kfn, vfn, kdn, vdn]
    for arr in (fk, fv, lc, dk, dv):
        for gi in range(g_pages):
            in_specs.append(paged(arr.shape[1:], gi))
            args.append(arr)
    in_specs += [seq((H_FOX, LANES)),
                 pl.BlockSpec((4, DH_DIFF), lambda b, p, pt: (0, 0)),
                 pl.BlockSpec((1, LANES), lambda b, p, pt: (0, 0))]
    args += [logfn, lamv, g]
    grid_spec = pltpu.PrefetchScalarGridSpec(
        num_scalar_prefetch=1,
        grid=(bs, n_steps + 1),
        in_specs=in_specs,
        out_specs=seq((t_new, D_MIX_AB)),
        scratch_shapes=[pltpu.VMEM((nblk * 2 * t_new, LANES), BF16),
                        pltpu.VMEM((nblk * 2 * t_new, 1), F32),
                        pltpu.VMEM((nblk * 2 * t_new, 1), F32),
                        pltpu.VMEM((nblk * 2 * t_new, LANES), F32),
                        pltpu.VMEM((H_FOX, LANES), F32)],
    )
    return pl.pallas_call(
        functools.partial(_attn_decode_kernel, g_pages=g_pages, n_steps=n_steps, t_new=t_new, lam_init=lam_init),
        grid_spec=grid_spec,
        out_shape=jax.ShapeDtypeStruct((bs, t_new, D_MIX_AB), BF16),
        compiler_params=_params(2),
        name="attn_decode",
    )(page_table, *args)


def _ffn_kernel(*refs, tm, tf, tiles_per_seq, paged_prev, final_norm):
    it = iter(refs)
    x_ref, mix_ref, wo_ref, g_ref, wa_ref, wb_ref, cw_ref, cb_ref, wd_ref = (next(it) for _ in range(9))
    if paged_prev:
        p1_ref, p2_ref = next(it), next(it)
    if final_norm:
        fg_ref = next(it)
    o_ref, nb_ref = next(it), next(it)
    if not paged_prev:
        carry_s = next(it)

        @pl.when((pl.program_id(0) % tiles_per_seq) == 0)
        def _():
            carry_s[...] = jnp.zeros(carry_s.shape, F32)

    x1 = x_ref[...] + _dot(mix_ref[...], wo_ref[...])
    h = _rms(x1, g_ref[...]).astype(BF16)
    nf = D_FF // tf
    row = lax.broadcasted_iota(jnp.int32, (tm if paged_prev else SUBLANES, tf), 0)

    def up(f):
        cols = slice(f * tf, (f + 1) * tf)
        return _dot(h, wa_ref[:, cols]), _dot(h, wb_ref[:, cols])

    def act(f, a, gate):
        cols = slice(f * tf, (f + 1) * tf)
        a1 = pltpu.roll(a, 1, 0)
        a2 = pltpu.roll(a, 2, 0)
        if paged_prev:
            t = row & (SUBLANES - 1)
            a1 = jnp.where(t == 0, p1_ref[:, cols], a1)
            a2 = jnp.where(t < 2, p2_ref[:, cols], a2)
            nb_ref[:, cols] = a
        else:
            c = carry_s[:, cols]
            top1 = jnp.where(row == 0, c[SUBLANES - 1:], a1[:SUBLANES])
            top2 = jnp.where(row == 0, c[SUBLANES - 2:SUBLANES - 1],
                             jnp.where(row == 1, c[SUBLANES - 1:], a2[:SUBLANES]))
            a1 = jnp.concatenate([top1, a1[SUBLANES:]], axis=0)
            a2 = jnp.concatenate([top2, a2[SUBLANES:]], axis=0)
            carry_s[:, cols] = a[tm - SUBLANES:, :]
            nb_ref[0, :, cols] = a[tm - (FFN_CONV_W - 1):, :]
        cw = cw_ref[:, cols]
        ac = cb_ref[:, cols] + a2 * cw[0:1] + a1 * cw[1:2] + a * cw[2:3]
        return (_gelu(ac) * gate).astype(BF16)

    acc = x1
    pending = up(0)
    for f in range(nf):
        following = up(f + 1) if f + 1 < nf else None
        acc = _dot(act(f, *pending), wd_ref[f * tf:(f + 1) * tf, :]) + acc
        pending = following
    if final_norm:
        acc = _rms(acc, fg_ref[...])
    o_ref[...] = acc


def _ffn(x2d, mix, wo, g, wa, wb, cw, cb, wd, b, t, tm, tf, prev=None, final_g=None):
    m = x2d.shape[0]
    kmix = mix.shape[1]
    paged_prev = prev is not None
    tiles_per_seq = max(t // tm, 1)
    rows = lambda n: pl.BlockSpec((tm, n), lambda i: (i, 0))
    in_specs = [rows(D_MODEL), rows(kmix), _resident((kmix, D_MODEL)), _resident((1, D_MODEL)),
                _resident((D_MODEL, D_FF)), _resident((D_MODEL, D_FF)), _resident((FFN_CONV_W, D_FF)),
                _resident((1, D_FF)), _resident((D_FF, D_MODEL))]
    args = [x2d, mix, wo, g, wa, wb, cw, cb, wd]
    scratch = []
    if paged_prev:
        in_specs += [rows(D_FF)] * 2
        args += list(prev)
        nb_spec = rows(D_FF)
        nb_shape = jax.ShapeDtypeStruct((m, D_FF), F32)
    else:
        nb_spec = pl.BlockSpec((1, FFN_CONV_W - 1, D_FF), lambda i: (i, 0, 0))
        nb_shape = jax.ShapeDtypeStruct((m // tm, FFN_CONV_W - 1, D_FF), F32)
        scratch.append(pltpu.VMEM((SUBLANES, D_FF), F32))
    if final_g is not None:
        in_specs.append(_resident((1, D_MODEL)))
        args.append(final_g)
    return pl.pallas_call(
        functools.partial(_ffn_kernel, tm=tm, tf=tf, tiles_per_seq=tiles_per_seq,
                          paged_prev=paged_prev, final_norm=final_g is not None),
        grid=(m // tm,),
        in_specs=in_specs,
        out_specs=(rows(D_MODEL), nb_spec),
        out_shape=(jax.ShapeDtypeStruct((m, D_MODEL), F32), nb_shape),
        scratch_shapes=scratch,
        compiler_params=_params(1),
        name="ffn",
    )(*args)


def _lru_kernel(*refs, tm, tiles_per_seq, paged_prev):
    it = iter(refs)
    x_ref, g_ref, wg_ref, wx_ref, cw_ref, cb_ref, wri_ref, bri_ref, lam_ref = (next(it) for _ in range(9))
    if paged_prev:
        p1_ref, p2_ref, p3_ref, h0_ref = (next(it) for _ in range(4))
    mix_ref, u_ref, hl_ref = next(it), next(it), next(it)
    a_s, bx_s, hs_s = next(it), next(it), next(it)
    if not paged_prev:
        cu_s, ch_s = next(it), next(it)
    mi = pl.program_id(0)

    h = _rms(x_ref[...], g_ref[...]).astype(BF16)
    u = _dot(h, wx_ref[...])
    row = lax.broadcasted_iota(jnp.int32, (tm, D_RNN), 0)
    if paged_prev:
        t = row & (SUBLANES - 1)
        u1 = jnp.where(t == 0, p1_ref[...], pltpu.roll(u, 1, 0))
        u2 = jnp.where(t < 2, p2_ref[...], pltpu.roll(u, 2, 0))
        u3 = jnp.where(t < 3, p3_ref[...], pltpu.roll(u, 3, 0))
        u_ref[...] = u
    else:
        @pl.when((mi % tiles_per_seq) == 0)
        def _():
            cu_s[...] = jnp.zeros(cu_s.shape, F32)
            ch_s[...] = jnp.zeros(ch_s.shape, F32)

        c = cu_s[...]
        l1, l2, l3 = (c[SUBLANES - k:SUBLANES - k + 1] for k in (1, 2, 3))
        u1 = jnp.where(row == 0, l1, pltpu.roll(u, 1, 0))
        u2 = jnp.where(row == 0, l2, jnp.where(row == 1, l1, pltpu.roll(u, 2, 0)))
        u3 = jnp.where(row == 0, l3, jnp.where(row == 1, l2, jnp.where(row == 2, l1, pltpu.roll(u, 3, 0))))
        cu_s[...] = u[tm - SUBLANES:, :]
        u_ref[0] = u[tm - (LRU_CONV_W - 1):, :]
    cw = cw_ref[...]
    xc = cb_ref[...] + u3 * cw[0:1] + u2 * cw[1:2] + u1 * cw[2:3] + u * cw[3:4]
    ri = _dot(xc.astype(BF16), wri_ref[...]) + bri_ref[...]
    rg = _sigmoid(ri[:, :D_RNN])
    ig = _sigmoid(ri[:, D_RNN:])
    log_a = LRU_C * rg * _log_sigmoid(lam_ref[...])
    av = jnp.exp(log_a)
    a_s[...] = av
    bx_s[...] = jnp.sqrt(-jnp.tanh(log_a) * (av * av + 1.0)) * (ig * xc)

    r8 = lax.broadcasted_iota(jnp.int32, (SUBLANES, D_RNN), 0)

    def group(gi, hc):
        sl = pl.ds(pl.multiple_of(gi * SUBLANES, SUBLANES), SUBLANES)
        av = a_s[sl, :]
        bv = bx_s[sl, :]
        for s in (1, 2, 4):
            keep = r8 >= s
            bv = jnp.where(keep, av * pltpu.roll(bv, s, 0) + bv, bv)
            av = jnp.where(keep, av * pltpu.roll(av, s, 0), av)
        if paged_prev:
            hc = h0_ref[pl.ds(gi, 1), :]
        hs = av * hc + bv
        hs_s[sl, :] = hs
        return hs[SUBLANES - 1:SUBLANES, :]

    if paged_prev:
        hc0 = jnp.zeros((1, D_RNN), F32)
    else:
        hc0 = ch_s[SUBLANES - 1:SUBLANES, :]
    lax.fori_loop(0, tm // SUBLANES, group, hc0)
    hs = hs_s[...]
    if paged_prev:
        hl_ref[...] = hs
    else:
        ch_s[...] = hs[tm - SUBLANES:, :]
        hl_ref[0] = hs[tm - 1:, :]
    gate = _gelu(_dot(h, wg_ref[...]))
    mix_ref[...] = (hs * gate).astype(BF16)


def _lru(x2d, g, wg, wx, cw, cb, wri, bri, lam, b, t, tm, prev=None):
    m = x2d.shape[0]
    paged_prev = prev is not None
    tiles_per_seq = max(t // tm, 1)
    rows = lambda n: pl.BlockSpec((tm, n), lambda i: (i, 0))
    in_specs = [rows(D_MODEL), _full((1, D_MODEL)), _full((D_MODEL, D_RNN)), _full((D_MODEL, D_RNN)),
                _full((LRU_CONV_W, D_RNN)), _full((1, D_RNN)), _full((D_RNN, 2 * D_RNN)), _full((1, 2 * D_RNN)),
                _full((1, D_RNN))]
    args = [x2d, g, wg, wx, cw, cb, wri, bri, lam]
    scratch = [pltpu.VMEM((tm, D_RNN), F32)] * 3
    if paged_prev:
        in_specs += [rows(D_RNN)] * 3 + [_full(prev[3].shape)]
        args += list(prev)
        u_spec, hl_spec = rows(D_RNN), rows(D_RNN)
        u_shape = hl_shape = jax.ShapeDtypeStruct((m, D_RNN), F32)
    else:
        u_spec = pl.BlockSpec((1, LRU_CONV_W - 1, D_RNN), lambda i: (i, 0, 0))
        hl_spec = pl.BlockSpec((1, 1, D_RNN), lambda i: (i, 0, 0))
        u_shape = jax.ShapeDtypeStruct((m // tm, LRU_CONV_W - 1, D_RNN), F32)
        hl_shape = jax.ShapeDtypeStruct((m // tm, 1, D_RNN), F32)
        scratch += [pltpu.VMEM((SUBLANES, D_RNN), F32)] * 2
    return pl.pallas_call(
        functools.partial(_lru_kernel, tm=tm, tiles_per_seq=tiles_per_seq, paged_prev=paged_prev),
        grid=(m // tm,),
        in_specs=in_specs,
        out_specs=(rows(D_RNN), u_spec, hl_spec),
        out_shape=(jax.ShapeDtypeStruct((m, D_RNN), BF16), u_shape, hl_shape),
        scratch_shapes=scratch,
        compiler_params=_params(1),
        name="lru",
    )(*args)


def _rope_tables(pos):
    half = ROT_DIMS // 2
    inv = ROPE_THETA ** (-jnp.arange(half, dtype=F32) / half)
    ang = pos.astype(F32)[:, None] * inv[None, :]
    lane = jnp.arange(LANES) % DH_DIFF
    cos = jnp.cos(ang)[:, lane % half]
    sin = jnp.sin(ang)[:, lane % half]
    cos_t = jnp.where(lane[None, :] < ROT_DIMS, cos, 1.0)
    sin_t = jnp.where(lane[None, :] < half, -sin, jnp.where(lane[None, :] < ROT_DIMS, sin, 0.0))
    return cos_t, sin_t


def _block_diag(w):
    n, c, d = w.shape
    idx = jnp.arange(n)
    return jnp.zeros((n, c, n, d), w.dtype).at[idx, :, idx, :].set(w).reshape(n * c, n * d)


def _expand_prev(buf, t, k):
    bs, wm1, c = buf.shape
    out = jnp.zeros((bs, t, c), buf.dtype)
    for step in range(min(k, t)):
        out = out.at[:, step].set(buf[:, wm1 - k + step])
    return out.reshape(bs * t, c)


def kernel(x_prompt, x_sample, cache_fox_k, cache_fox_v, cache_fox_logf, cache_diff_k, cache_diff_v, state_lru_conv, state_lru_h, state_ffn_conv, page_table, mix_norm_g, ab_w_in, ab_b_f, ab_lam_q1, ab_lam_k1, ab_lam_q2, ab_lam_k2, ab_subln_g, ab_w_out, lru_w_gate, lru_w_x, lru_conv_w, lru_conv_b, lru_w_a, lru_b_a, lru_w_i, lru_b_i, lru_lambda, lru_w_out, ffn_norm_g, ffn_w_a, ffn_w_b, ffn_conv_w, ffn_conv_b, ffn_w_down, final_norm_g):
    depth = mix_norm_g.shape[0]
    assert depth == 2 and ab_w_in.shape[0] == 1 and lru_w_x.shape[0] == 1

    w_in = ab_w_in[0]
    o_f = 3 * D_FOX
    w_all = jnp.concatenate([w_in[:, :o_f], w_in[:, o_f + H_FOX:], w_in[:, o_f:o_f + H_FOX],
                             jnp.zeros((D_MODEL, LANES - H_FOX), F32)], axis=1).astype(BF16)
    bf_pad = jnp.pad(ab_b_f[0], (0, LANES - H_FOX)).reshape(1, LANES)
    lamv = jnp.stack([ab_lam_q1[0], ab_lam_k1[0], ab_lam_q2[0], ab_lam_k2[0]]).astype(F32)
    subln_g = ab_subln_g[0].reshape(1, LANES)
    w_out = ab_w_out[0].astype(BF16)
    wg = lru_w_gate[0].astype(BF16)
    wx = lru_w_x[0].astype(BF16)
    wri = jnp.concatenate([_block_diag(lru_w_a[0]), _block_diag(lru_w_i[0])], axis=1).astype(BF16)
    bri = jnp.concatenate([lru_b_a[0], lru_b_i[0]]).reshape(1, 2 * D_RNN)
    lru_wo = lru_w_out[0].astype(BF16)
    ffn_wa = ffn_w_a.astype(BF16)
    ffn_wb = ffn_w_b.astype(BF16)
    ffn_wd = ffn_w_down.astype(BF16)
    lam_init = 0.8 - 0.6 * math.exp(-0.3 * 0)

    def layer0_proj(x2d, pos_rows, tm):
        cos_t, sin_t = _rope_tables(pos_rows)
        return _inproj(x2d, mix_norm_g[0].reshape(1, D_MODEL), w_all, bf_pad, cos_t, sin_t, tm)

    def tail(x2d, mix0, b, t, tm_ffn, tm_lru, tf, ffn_prev, lru_prev):
        x1, nb0 = _ffn(x2d, mix0, w_out, ffn_norm_g[0].reshape(1, D_MODEL), ffn_wa[0], ffn_wb[0],
                       ffn_conv_w[0], ffn_conv_b[0].reshape(1, D_FF), ffn_wd[0], b, t, tm_ffn, tf,
                       prev=ffn_prev[0])
        mix1, u_out, h_out = _lru(x1, mix_norm_g[1].reshape(1, D_MODEL), wg, wx, lru_conv_w[0],
                                  lru_conv_b[0].reshape(1, D_RNN), wri, bri, lru_lambda[0].reshape(1, D_RNN),
                                  b, t, tm_lru, prev=lru_prev)
        y, nb1 = _ffn(x1, mix1, lru_wo, ffn_norm_g[1].reshape(1, D_MODEL), ffn_wa[1], ffn_wb[1],
                      ffn_conv_w[1], ffn_conv_b[1].reshape(1, D_FF), ffn_wd[1], b, t, tm_ffn, tf,
                      prev=ffn_prev[1], final_g=final_norm_g.reshape(1, D_MODEL))
        return y, nb0, nb1, u_out, h_out

    bp, tp, _ = x_prompt.shape
    mp = bp * tp
    xp = x_prompt.reshape(mp, D_MODEL)
    (qf, kf, vf, kfb, vfb, logf, logft, qd, kd, vd, kdb, vdb) = layer0_proj(xp, jnp.arange(tp), 512)
    cum = _cumsum_prompt(logft, bp, tp)
    r3 = lambda a: a.reshape(bp, tp, D_FOX)
    tq = 512
    of = _attn_prompt(r3(qf), r3(kfb), r3(vfb), (cum,), True, tq, lam_init)
    od = _attn_prompt(r3(qd), r3(kdb), r3(vdb), (lamv, subln_g), False, tq, lam_init)
    mix0 = jnp.concatenate([of, od], axis=-1).reshape(mp, D_MIX_AB)
    tm_ffn, tm_lru = 256, 256
    y_p, nb0, nb1, u_out, h_out = tail(xp, mix0, bp, tp, tm_ffn, tm_lru, 512, (None, None), None)
    seq_last = lambda a, tm: a[tp // tm - 1::tp // tm]
    nb0, nb1, u_out, h_out = seq_last(nb0, tm_ffn), seq_last(nb1, tm_ffn), seq_last(u_out, tm_lru), seq_last(h_out, tm_lru)
    y_prompt = y_p.reshape(bp, tp, D_MODEL)
    p_state = (kf.reshape(1, bp, tp, H_FOX, DH_FOX), vf.reshape(1, bp, tp, H_FOX, DH_FOX),
               logf.reshape(1, bp, tp, H_FOX),
               kd.reshape(1, bp, tp, H_DIFF, 2 * DH_DIFF), vd.reshape(1, bp, tp, H_DIFF, 2 * DH_DIFF),
               u_out[None], h_out.reshape(1, bp, D_RNN), jnp.stack([nb0, nb1]))

    bs, ts, _ = x_sample.shape
    assert ts == SUBLANES
    ms = bs * ts
    n_pages = page_table.shape[1]
    past = n_pages * PAGE_SIZE
    xs = x_sample.reshape(ms, D_MODEL)
    pos_rows = jnp.tile(past + jnp.arange(ts), bs)
    (qf, kf, vf, kfb, vfb, logf, logft, qd, kd, vd, kdb, vdb) = layer0_proj(xs, pos_rows, ms)
    n_pool = cache_fox_k.shape[1]
    logf_pool = jnp.swapaxes(cache_fox_logf[0], 1, 2).reshape(n_pool * H_FOX, PAGE_SIZE)
    lc = _cumsum_rows(logf_pool).reshape(n_pool, H_FOX, PAGE_SIZE)
    logfn = jnp.pad(logft.reshape(H_FOX, bs, ts).transpose(1, 0, 2), ((0, 0), (0, 0), (0, LANES - ts)))
    pad_rows = lambda a: jnp.pad(a.reshape(bs, ts, D_FOX), ((0, 0), (0, PAGE_SIZE - ts), (0, 0)))
    kt_pages = lambda c: jnp.transpose(c, (0, 2, 3, 1)).reshape(n_pool, D_FOX, PAGE_SIZE)
    mix0 = _attn_decode(page_table, qf.astype(F32).reshape(bs, ts, D_FOX), qd.astype(F32).reshape(bs, ts, D_DIFF),
                        pad_rows(kfb), pad_rows(vfb), pad_rows(kdb), pad_rows(vdb),
                        kt_pages(cache_fox_k[0]), kt_pages(cache_fox_v[0]), lc,
                        cache_diff_k[0].reshape(n_pool, PAGE_SIZE * H_DIFF, 2 * DH_DIFF),
                        cache_diff_v[0].reshape(n_pool, PAGE_SIZE * H_DIFF, 2 * DH_DIFF),
                        logfn, lamv, subln_g, 8, lam_init).reshape(ms, D_MIX_AB)
    ffn_prev = tuple((_expand_prev(state_ffn_conv[l], ts, 1), _expand_prev(state_ffn_conv[l], ts, 2))
                     for l in range(depth))
    lru_prev = tuple(_expand_prev(state_lru_conv[0], ts, k) for k in (1, 2, 3)) + (state_lru_h[0],)
    y_s, a0, a1, u_full, h_full = tail(xs, mix0, bs, ts, ms, ms, 512, ffn_prev, lru_prev)
    y_sample = y_s.reshape(bs, ts, D_MODEL)
    last = lambda a, k: a.reshape(bs, ts, -1)[:, ts - k:]
    s_state = (kf.reshape(1, bs, ts, H_FOX, DH_FOX), vf.reshape(1, bs, ts, H_FOX, DH_FOX),
               logf.reshape(1, bs, ts, H_FOX),
               kd.reshape(1, bs, ts, H_DIFF, 2 * DH_DIFF), vd.reshape(1, bs, ts, H_DIFF, 2 * DH_DIFF),
               last(u_full, LRU_CONV_W - 1)[None], last(h_full, 1).reshape(1, bs, D_RNN),
               jnp.stack([last(a0, FFN_CONV_W - 1), last(a1, FFN_CONV_W - 1)]))

    return (y_prompt, y_sample) + p_state + s_state
```

```python
import functools
import math

import jax
import jax.numpy as jnp
from jax import lax
from jax.experimental import pallas as pl
from jax.experimental.pallas import tpu as pltpu

F32 = jnp.float32
BF16 = jnp.bfloat16

D_MODEL = 1024
H_FOX = 8
DH_FOX = 64
H_DIFF = 4
DH_DIFF = 64
D_FOX = H_FOX * DH_FOX
D_DIFF = H_DIFF * 2 * DH_DIFF
D_MIX_AB = D_FOX + D_DIFF
ROT_DIMS = DH_DIFF // 4
ROPE_THETA = 500000.0
D_RNN = 1280
N_LRU_BLOCKS = 16
LRU_BW = D_RNN // N_LRU_BLOCKS
LRU_CONV_W = 4
LRU_C = 8.0
D_FF = 3072
FFN_CONV_W = 3
EPS = 1e-6
NEG = -1e30
PAGE_SIZE = 128
LOG2E = math.log2(math.e)

LANES = 128
SUBLANES = 8
VMEM_LIMIT = 56 * 1024 * 1024
N_QKV = 6 * D_FOX
N_INPROJ = N_QKV + LANES


def _params(n_axes):
    return pltpu.CompilerParams(dimension_semantics=("arbitrary",) * n_axes,
                                vmem_limit_bytes=VMEM_LIMIT)


def _full(shape):
    n = len(shape)
    return pl.BlockSpec(shape, lambda *_: (0,) * n)


def _resident(shape):
    n = len(shape)
    return pl.BlockSpec(shape, lambda *_: (0,) * n, pipeline_mode=pl.Buffered(1))


def _rms(x, g):
    return x * lax.rsqrt(jnp.mean(x * x, axis=-1, keepdims=True) + EPS) * g


def _gelu(x):
    c = math.sqrt(2.0 / math.pi)
    return x * (0.5 * (1.0 + jnp.tanh(c * (x + 0.044715 * (x * x * x)))))


def _log_sigmoid(z):
    return -(jnp.maximum(-z, 0.0) + jnp.log1p(jnp.exp(-jnp.abs(z))))


def _sigmoid(z):
    return 1.0 / (1.0 + jnp.exp(-z))


def _dot(a, b):
    return jnp.dot(a, b, preferred_element_type=F32)


def _dot_nt(a, b):
    return lax.dot_general(a, b, (((1,), (1,)), ((), ())), preferred_element_type=F32)


def _prefix_lanes(x):
    r = lax.broadcasted_iota(jnp.int32, (LANES, LANES), 0)
    c = lax.broadcasted_iota(jnp.int32, (LANES, LANES), 1)
    tri = jnp.where(r <= c, 1.0, 0.0).astype(BF16)
    hi = x.astype(BF16)
    r1 = x - hi.astype(F32)
    mid = r1.astype(BF16)
    lo = (r1 - mid.astype(F32)).astype(BF16)
    return _dot(hi, tri) + _dot(mid, tri) + _dot(lo, tri)


def _inproj_kernel(x_ref, g_ref, w_ref, bf_ref, cos_ref, sin_ref,
                   qf_ref, kf_ref, vf_ref, kfb_ref, vfb_ref, logf_ref, logft_ref,
                   qd_ref, kd_ref, vd_ref, kdb_ref, vdb_ref):
    h = _rms(x_ref[...], g_ref[...]).astype(BF16)

    def proj(i, width=D_FOX):
        return _dot(h, w_ref[:, i * D_FOX:i * D_FOX + width])

    qf_ref[...] = (proj(0) * (DH_FOX ** -0.5 * LOG2E)).astype(BF16)
    kf = proj(1)
    kf_ref[...] = kf
    kfb_ref[...] = kf.astype(BF16)
    vf = proj(2)
    vf_ref[...] = vf
    vfb_ref[...] = vf.astype(BF16)

    logf = _log_sigmoid(proj(6, LANES) + bf_ref[...])
    logf_ref[...] = logf[:, :H_FOX]
    logft_ref[...] = logf.T[:H_FOX, :]

    cosf = jnp.concatenate([cos_ref[...]] * (D_DIFF // LANES), axis=1)
    sinf = jnp.concatenate([sin_ref[...]] * (D_DIFF // LANES), axis=1)
    lane = lax.broadcasted_iota(jnp.int32, cosf.shape, 1) & (DH_DIFF - 1)
    half = ROT_DIMS // 2

    def rope(x):
        partner = jnp.where(lane < half, pltpu.roll(x, D_DIFF - half, 1), pltpu.roll(x, half, 1))
        return jnp.where(lane < ROT_DIMS, x * cosf + partner * sinf, x)

    qd_ref[...] = (rope(proj(3)) * (DH_DIFF ** -0.5 * LOG2E)).astype(BF16)
    kd = rope(proj(4))
    kd_ref[...] = kd
    kdb_ref[...] = kd.astype(BF16)
    vd = proj(5)
    vd_ref[...] = vd
    vdb_ref[...] = vd.astype(BF16)


def _inproj(x2d, g, w_all, bf_pad, cos_t, sin_t, tm):
    m = x2d.shape[0]
    nt = cos_t.shape[0] // tm
    row = lambda n: pl.BlockSpec((tm, n), lambda i: (i, 0))
    tab = pl.BlockSpec((tm, LANES), lambda i: (i % nt, 0))
    f32o = jax.ShapeDtypeStruct((m, D_FOX), F32)
    b16o = jax.ShapeDtypeStruct((m, D_FOX), BF16)
    out_shape = (b16o, f32o, f32o, b16o, b16o,
                 jax.ShapeDtypeStruct((m, H_FOX), F32), jax.ShapeDtypeStruct((H_FOX, m), F32),
                 b16o, f32o, f32o, b16o, b16o)
    out_specs = (row(D_FOX),) * 5 + (row(H_FOX), pl.BlockSpec((H_FOX, tm), lambda i: (0, i))) + (row(D_FOX),) * 5
    return pl.pallas_call(
        _inproj_kernel,
        grid=(m // tm,),
        in_specs=[row(D_MODEL), _full((1, D_MODEL)), _full((D_MODEL, N_INPROJ)), _full((1, LANES)), tab, tab],
        out_specs=out_specs,
        out_shape=out_shape,
        compiler_params=_params(1),
        name="inproj",
    )(x2d, g, w_all, bf_pad, cos_t, sin_t)


def _cumsum_prompt_kernel(x_ref, o_ref, *, t):
    carry = jnp.zeros((H_FOX, 1), F32)
    for i in range(t // LANES):
        c = _prefix_lanes(x_ref[:, i * LANES:(i + 1) * LANES]) + carry
        c2 = c * LOG2E
        for j in range(H_FOX // 2):
            o_ref[0, j, :, i * LANES:(i + 1) * LANES] = c2[2 * j:2 * j + 2, :]
        carry = c[:, LANES - 1:LANES]


def _cumsum_prompt(logft, b, t):
    return pl.pallas_call(
        functools.partial(_cumsum_prompt_kernel, t=t),
        grid=(b,),
        in_specs=[pl.BlockSpec((H_FOX, t), lambda i: (0, i))],
        out_specs=pl.BlockSpec((1, H_FOX // 2, 2, t), lambda i: (i, 0, 0, 0)),
        out_shape=jax.ShapeDtypeStruct((b, H_FOX // 2, 2, t), F32),
        compiler_params=_params(1),
        name="cumsum_prompt",
    )(logft)


def _cumsum_rows_kernel(x_ref, o_ref):
    o_ref[...] = _prefix_lanes(x_ref[...])


def _cumsum_rows(x, max_rows=2048):
    rows = x.shape[0]
    tr = max(d for d in range(SUBLANES, max_rows + 1, SUBLANES) if rows % d == 0)
    spec = pl.BlockSpec((tr, LANES), lambda i: (i, 0))
    return pl.pallas_call(
        _cumsum_rows_kernel,
        grid=(rows // tr,),
        in_specs=[spec],
        out_specs=spec,
        out_shape=jax.ShapeDtypeStruct((rows, LANES), F32),
        compiler_params=_params(1),
        name="cumsum_pages",
    )(x)


def _stack_masked(q):
    lane = lax.broadcasted_iota(jnp.int32, q.shape, 1)
    zero = jnp.zeros_like(q)
    return jnp.concatenate([jnp.where(lane < LANES // 2, q, zero),
                            jnp.where(lane >= LANES // 2, q, zero)], axis=0)


def _lambda(lamv, lam_init):
    s1 = jnp.sum(lamv[0:1] * lamv[1:2], axis=1, keepdims=True)
    s2 = jnp.sum(lamv[2:3] * lamv[3:4], axis=1, keepdims=True)
    return jnp.exp(s1) - jnp.exp(s2) + lam_init


def _fox_out(acc, l, r):
    o = acc / l
    lane = lax.broadcasted_iota(jnp.int32, (r, LANES), 1)
    return jnp.where(lane < LANES // 2, o[:r], o[r:])


def _diff_out(acc, l, r, lamv, g, lam_init):
    o = acc / l
    od = o[:r] - _lambda(lamv, lam_init) * o[r:]
    return _rms(od, g) * (1.0 - lam_init)


def _attn_prompt_kernel(*refs, is_fox, tq, rc, lam_init):
    if is_fox:
        q_ref, k_ref, v_ref, c_ref, o_ref = refs[:5]
    else:
        q_ref, k_ref, v_ref, lamv_ref, g_ref, o_ref = refs[:6]
    s0_s, s1_s, r0_s, r1_s, m_s, l_s, acc_s = refs[-7:]
    qi = pl.program_id(2)
    q2 = _stack_masked(q_ref[0])
    nrep = tq // LANES

    def keys(kb):
        return pl.ds(pl.multiple_of(kb * tq, tq), tq)

    def row_max(s):
        return jnp.broadcast_to(jnp.max(s, axis=1, keepdims=True), (2 * tq, LANES))

    def qk(kb, s_s, r_s):
        s = _dot_nt(q2, k_ref[0, keys(kb), :])
        if is_fox:
            c0 = c_ref[0, 0, 0:1, keys(kb)]
            c1 = c_ref[0, 0, 1:2, keys(kb)]
            s = jnp.concatenate([s[:tq] - c0, s[tq:] - c1], axis=0)
        s_s[...] = s
        r_s[...] = row_max(s)

    def softmax_pv(kb, s_s, r_s):
        m_old = m_s[...]
        m_new = jnp.maximum(m_old, r_s[...])
        alpha = jnp.exp2(m_old - m_new)
        m_s[...] = m_new
        sums, probs = [], []
        for ci in range(2 * tq // rc):
            rows = slice(ci * rc, (ci + 1) * rc)
            p = jnp.exp2(s_s[rows, :] - jnp.tile(m_new[rows], (1, nrep)))
            sums.append(sum(p[:, i * LANES:(i + 1) * LANES] for i in range(nrep)))
            probs.append(p.astype(BF16))
        l_s[...] = alpha * l_s[...] + jnp.concatenate(sums, axis=0)
        acc_s[...] = alpha * acc_s[...] + _dot(jnp.concatenate(probs, axis=0), v_ref[0, keys(kb), :])

    m_s[...] = jnp.full(m_s.shape, NEG, F32)
    l_s[...] = jnp.zeros(l_s.shape, F32)
    acc_s[...] = jnp.zeros(acc_s.shape, F32)
    bufs = ((s0_s, r0_s), (s1_s, r1_s))
    qk(0, *bufs[0])

    def body(kb, _):
        for parity in range(2):
            @pl.when((kb & 1) == parity)
            def _():
                qk(kb + 1, *bufs[1 - parity])
                softmax_pv(kb, *bufs[parity])
        return 0

    lax.fori_loop(0, qi, body, 0)
    row = lax.broadcasted_iota(jnp.int32, (2 * tq, tq), 0)
    col = lax.broadcasted_iota(jnp.int32, (2 * tq, tq), 1)
    causal = col <= jnp.where(row >= tq, row - tq, row)
    for parity in range(2):
        @pl.when((qi & 1) == parity)
        def _():
            s_s, r_s = bufs[parity]
            s = jnp.where(causal, s_s[...], NEG)
            s_s[...] = s
            r_s[...] = row_max(s)
            softmax_pv(qi, s_s, r_s)

    acc = acc_s[...]
    l = jnp.sum(l_s[...], axis=1, keepdims=True)
    if is_fox:
        o_ref[0] = _fox_out(acc, l, tq).astype(BF16)
    else:
        o_ref[0] = _diff_out(acc, l, tq, lamv_ref[...], g_ref[...], lam_init).astype(BF16)


def _attn_prompt(q, k, v, extra, is_fox, tq, lam_init, rc=32):
    b, t, _ = q.shape
    nblk = D_FOX // LANES
    wide = pltpu.VMEM((2 * tq, LANES), F32)
    scratch = [pltpu.VMEM((2 * tq, tq), F32), pltpu.VMEM((2 * tq, tq), F32), wide, wide, wide, wide, wide]
    qspec = pl.BlockSpec((1, tq, LANES), lambda bi, j, qi: (bi, qi, j))
    kvspec = pl.BlockSpec((1, t, LANES), lambda bi, j, qi: (bi, 0, j))
    if is_fox:
        especs = [pl.BlockSpec((1, 1, 2, t), lambda bi, j, qi: (bi, j, 0, 0))]
    else:
        especs = [_full((4, DH_DIFF)), _full((1, LANES))]
    return pl.pallas_call(
        functools.partial(_attn_prompt_kernel, is_fox=is_fox, tq=tq, rc=rc, lam_init=lam_init),
        grid=(b, nblk, t // tq),
        in_specs=[qspec, kvspec, kvspec] + especs,
        out_specs=qspec,
        out_shape=jax.ShapeDtypeStruct((b, t, D_FOX), BF16),
        scratch_shapes=scratch,
        compiler_params=_params(3),
        name="attn_prompt_fox" if is_fox else "attn_prompt_diff",
    )(q, k, v, *extra)


def _attn_decode_kernel(pt_ref, *refs, g_pages, n_steps, t_new, lam_init):
    n_in = 6 + 5 * g_pages + 5
    (qf_ref, qd_ref, kfn_ref, vfn_ref, kdn_ref, vdn_ref) = refs[:6]
    pages = refs[6:6 + 5 * g_pages]
    fk_refs, fv_refs, lc_refs, dk_refs, dv_refs = (pages[i * g_pages:(i + 1) * g_pages] for i in range(5))
    logfn_ref, lamv_ref, g_ref = refs[6 + 5 * g_pages:6 + 5 * g_pages + 3]
    o_ref = refs[n_in - 2]
    q2_s, m_s, l_s, acc_s, cc_s = refs[n_in - 1:]
    del pt_ref
    p = pl.program_id(1)
    nfb = D_FOX // LANES
    ndb = D_DIFF // LANES
    r = SUBLANES

    nblk = nfb + ndb
    rb = 2 * r

    def blk(x, j):
        return x[j * rb:(j + 1) * rb]

    def lanes(x, j):
        return x[:, j * LANES:(j + 1) * LANES]

    @pl.when(p == 0)
    def _():
        q = jnp.concatenate([qf_ref[0], qd_ref[0]], axis=1)
        q2_s[...] = jnp.concatenate([_stack_masked(lanes(q, j)) for j in range(nblk)], axis=0).astype(BF16)
        m_s[...] = jnp.full(m_s.shape, NEG, F32)
        l_s[...] = jnp.zeros(l_s.shape, F32)
        acc_s[...] = jnp.zeros(acc_s.shape, F32)
        cc_s[...] = jnp.zeros(cc_s.shape, F32)

    def fox_bias(ck):
        ck = ck * LOG2E
        return jnp.concatenate([jnp.broadcast_to(ck[h:h + 1], (r, ck.shape[1])) for h in range(H_FOX)], axis=0)

    nf = nfb * rb
    nd = ndb * rb

    def update(lo, n, u, pv_fn):
        sl = slice(lo, lo + n)
        m = m_s[sl]
        m_new = jnp.maximum(m, jnp.max(u, axis=1, keepdims=True))
        alpha = jnp.exp2(m - m_new)
        pe = jnp.exp2(u - m_new)
        m_s[sl] = m_new
        l_s[sl] = alpha * l_s[sl] + jnp.sum(pe, axis=1, keepdims=True)
        acc_s[sl] = alpha * acc_s[sl] + pv_fn(pe.astype(BF16))

    @pl.when(p < n_steps)
    def _():
        cks = []
        run = cc_s[...]
        for gi in range(g_pages):
            lc = lc_refs[gi][0]
            cks.append(run + lc)
            run = run + lc[:, LANES - 1:LANES]
        cc_s[...] = run
        q2 = q2_s[...]

        def fkv(refs, gi, j):
            return refs[gi][0, j * LANES:(j + 1) * LANES, :].astype(BF16)

        s_fox = jnp.concatenate(
            [jnp.concatenate([_dot(blk(q2, j), fkv(fk_refs, gi, j)) for gi in range(g_pages)], axis=1)
             for j in range(nfb)], axis=0)

        def pv_fox(pb):
            return jnp.concatenate(
                [sum(_dot_nt(lanes(blk(pb, j), gi), fkv(fv_refs, gi, j)) for gi in range(g_pages))
                 for j in range(nfb)], axis=0)

        update(0, nf, s_fox - fox_bias(jnp.concatenate(cks, axis=1)), pv_fox)

        wd = PAGE_SIZE * H_DIFF
        s_diff = jnp.concatenate([_dot_nt(q2[nf:], dk_refs[gi][0].astype(BF16)) for gi in range(g_pages)], axis=1)
        row = lax.broadcasted_iota(jnp.int32, s_diff.shape, 0)
        col = lax.broadcasted_iota(jnp.int32, s_diff.shape, 1)
        own = (col & (H_DIFF - 1)) == row // rb

        def pv_diff(pb):
            return sum(_dot(pb[:, gi * wd:(gi + 1) * wd], dv_refs[gi][0].astype(BF16)) for gi in range(g_pages))

        update(nf, nd, jnp.where(own, s_diff, NEG), pv_diff)

    @pl.when(p == n_steps)
    def _():
        row = lax.broadcasted_iota(jnp.int32, (nf, LANES), 0) & (r - 1)
        col = lax.broadcasted_iota(jnp.int32, (nf, LANES), 1)
        keep = (col <= row) & (col < t_new)
        ck = cc_s[...] + _prefix_lanes(logfn_ref[0])
        q2 = q2_s[...]
        s_fox = jnp.concatenate([_dot_nt(blk(q2, j), lanes(kfn_ref[0], j)) for j in range(nfb)], axis=0)
        update(0, nf, jnp.where(keep, s_fox - fox_bias(ck), NEG),
               lambda pb: jnp.concatenate([_dot(blk(pb, j), lanes(vfn_ref[0], j)) for j in range(nfb)], axis=0))
        s_diff = jnp.concatenate([_dot_nt(blk(q2, nfb + j), lanes(kdn_ref[0], j)) for j in range(ndb)], axis=0)
        update(nf, nd, jnp.where(keep, s_diff, NEG),
               lambda pb: jnp.concatenate([_dot(blk(pb, j), lanes(vdn_ref[0], j)) for j in range(ndb)], axis=0))
        acc = acc_s[...]
        l = l_s[...]
        lamv = lamv_ref[...]
        for j in range(nfb):
            o_ref[0, :, j * LANES:(j + 1) * LANES] = _fox_out(blk(acc, j), blk(l, j), r).astype(BF16)
        for j in range(nfb, nblk):
            o = _diff_out(blk(acc, j), blk(l, j), r, lamv, g_ref[...], lam_init)
            o_ref[0, :, j * LANES:(j + 1) * LANES] = o.astype(BF16)


def _attn_decode(page_table, qf, qd, kfn, vfn, kdn, vdn, fk, fv, lc, dk, dv, logfn, lamv, g, g_pages, lam_init):
    bs, t_new, _ = qf.shape
    n_pages = page_table.shape[1]
    n_steps = n_pages // g_pages
    nblk = (D_FOX + D_DIFF) // LANES

    def seq(shape):
        return pl.BlockSpec((1,) + shape, lambda b, p, pt: (b, 0, 0))

    def paged(shape, gi):
        def index(b, p, pt):
            return (pt[b, jnp.minimum(p, n_steps - 1) * g_pages + gi],) + (0,) * len(shape)
        return pl.BlockSpec((1,) + shape, index)

    in_specs = [seq((t_new, D_FOX)), seq((t_new, D_DIFF))] + [seq((PAGE_SIZE, D_FOX))] * 4
    args = [qf, qd, kfn, vfn, kdn, vdn]
    for arr in (fk, fv, lc, dk, dv):
        for gi in range(g_pages):
            in_specs.append(paged(arr.shape[1:], gi))
            args.append(arr)
    in_specs += [seq((H_FOX, LANES)),
                 pl.BlockSpec((4, DH_DIFF), lambda b, p, pt: (0, 0)),
                 pl.BlockSpec((1, LANES), lambda b, p, pt: (0, 0))]
    args += [logfn, lamv, g]
    grid_spec = pltpu.PrefetchScalarGridSpec(
        num_scalar_prefetch=1,
        grid=(bs, n_steps + 1),
        in_specs=in_specs,
        out_specs=seq((t_new, D_MIX_AB)),
        scratch_shapes=[pltpu.VMEM((nblk * 2 * t_new, LANES), BF16),
                        pltpu.VMEM((nblk * 2 * t_new, 1), F32),
                        pltpu.VMEM((nblk * 2 * t_new, 1), F32),
                        pltpu.VMEM((nblk * 2 * t_new, LANES), F32),
                        pltpu.VMEM((H_FOX, LANES), F32)],
    )
    return pl.pallas_call(
        functools.partial(_attn_decode_kernel, g_pages=g_pages, n_steps=n_steps, t_new=t_new, lam_init=lam_init),
        grid_spec=grid_spec,
        out_shape=jax.ShapeDtypeStruct((bs, t_new, D_MIX_AB), BF16),
        compiler_params=_params(2),
        name="attn_decode",
    )(page_table, *args)


def _ffn_kernel(*refs, tm, tf, n_mix, tiles_per_seq, paged_prev, final_norm):
    it = iter(refs)
    x_ref = next(it)
    mix_refs = [next(it) for _ in range(n_mix)]
    wo_ref, g_ref, wa_ref, wb_ref, cw_ref, cb_ref, wd_ref = (next(it) for _ in range(7))
    if paged_prev:
        p1_ref, p2_ref = next(it), next(it)
    if final_norm:
        fg_ref = next(it)
    o_ref, nb_ref = next(it), next(it)
    if not paged_prev:
        carry_s = next(it)

        @pl.when((pl.program_id(0) % tiles_per_seq) == 0)
        def _():
            carry_s[...] = jnp.zeros(carry_s.shape, F32)

    y = None
    k0 = 0
    for mix_ref in mix_refs:
        k1 = k0 + mix_ref.shape[1]
        part = _dot(mix_ref[...], wo_ref[k0:k1, :])
        y = part if y is None else part + y
        k0 = k1
    x1 = x_ref[...] + y
    h = _rms(x1, g_ref[...]).astype(BF16)
    nf = D_FF // tf
    row = lax.broadcasted_iota(jnp.int32, (tm if paged_prev else SUBLANES, tf), 0)

    def up(f):
        cols = slice(f * tf, (f + 1) * tf)
        return _dot(h, wa_ref[:, cols]), _dot(h, wb_ref[:, cols])

    def act(f, a, gate):
        cols = slice(f * tf, (f + 1) * tf)
        a1 = pltpu.roll(a, 1, 0)
        a2 = pltpu.roll(a, 2, 0)
        if paged_prev:
            t = row & (SUBLANES - 1)
            a1 = jnp.where(t == 0, p1_ref[:, cols], a1)
            a2 = jnp.where(t < 2, p2_ref[:, cols], a2)
            nb_ref[:, cols] = a
        else:
            c = carry_s[:, cols]
            top1 = jnp.where(row == 0, c[SUBLANES - 1:], a1[:SUBLANES])
            top2 = jnp.where(row == 0, c[SUBLANES - 2:SUBLANES - 1],
                             jnp.where(row == 1, c[SUBLANES - 1:], a2[:SUBLANES]))
            a1 = jnp.concatenate([top1, a1[SUBLANES:]], axis=0)
            a2 = jnp.concatenate([top2, a2[SUBLANES:]], axis=0)
            carry_s[:, cols] = a[tm - SUBLANES:, :]
            nb_ref[0, :, cols] = a[tm - (FFN_CONV_W - 1):, :]
        cw = cw_ref[:, cols]
        ac = cb_ref[:, cols] + a2 * cw[0:1] + a1 * cw[1:2] + a * cw[2:3]
        return (_gelu(ac) * gate).astype(BF16)

    acc = x1
    pending = up(0)
    for f in range(nf):
        following = up(f + 1) if f + 1 < nf else None
        acc = _dot(act(f, *pending), wd_ref[f * tf:(f + 1) * tf, :]) + acc
        pending = following
    if final_norm:
        acc = _rms(acc, fg_ref[...])
    o_ref[...] = acc


def _ffn(x2d, mix_parts, wo, g, wa, wb, cw, cb, wd, b, t, tm, tf, prev=None, final_g=None):
    m = x2d.shape[0]
    paged_prev = prev is not None
    tiles_per_seq = max(t // tm, 1)
    rows = lambda n: pl.BlockSpec((tm, n), lambda i: (i, 0))
    in_specs = ([rows(D_MODEL)] + [rows(p.shape[1]) for p in mix_parts]
                + [_resident(wo.shape), _resident((1, D_MODEL)),
                   _resident((D_MODEL, D_FF)), _resident((D_MODEL, D_FF)), _resident((FFN_CONV_W, D_FF)),
                   _resident((1, D_FF)), _resident((D_FF, D_MODEL))])
    args = [x2d, *mix_parts, wo, g, wa, wb, cw, cb, wd]
    scratch = []
    if paged_prev:
        in_specs += [rows(D_FF)] * 2
        args += list(prev)
        nb_spec = rows(D_FF)
        nb_shape = jax.ShapeDtypeStruct((m, D_FF), F32)
    else:
        nb_spec = pl.BlockSpec((1, FFN_CONV_W - 1, D_FF), lambda i: (i, 0, 0))
        nb_shape = jax.ShapeDtypeStruct((m // tm, FFN_CONV_W - 1, D_FF), F32)
        scratch.append(pltpu.VMEM((SUBLANES, D_FF), F32))
    if final_g is not None:
        in_specs.append(_resident((1, D_MODEL)))
        args.append(final_g)
    return pl.pallas_call(
        functools.partial(_ffn_kernel, tm=tm, tf=tf, n_mix=len(mix_parts), tiles_per_seq=tiles_per_seq,
                          paged_prev=paged_prev, final_norm=final_g is not None),
        grid=(m // tm,),
        in_specs=in_specs,
        out_specs=(rows(D_MODEL), nb_spec),
        out_shape=(jax.ShapeDtypeStruct((m, D_MODEL), F32), nb_shape),
        scratch_shapes=scratch,
        compiler_params=_params(1),
        name="ffn",
    )(*args)


def _lru_kernel(*refs, tm, tiles_per_seq, paged_prev):
    it = iter(refs)
    x_ref, g_ref, wg_ref, wx_ref, cw_ref, cb_ref, wri_ref, bri_ref, lam_ref = (next(it) for _ in range(9))
    if paged_prev:
        p1_ref, p2_ref, p3_ref, h0_ref = (next(it) for _ in range(4))
    mix_ref, u_ref, hl_ref = next(it), next(it), next(it)
    if not paged_prev:
        cu_s, ch_s = next(it), next(it)
    mi = pl.program_id(0)

    h = _rms(x_ref[...], g_ref[...]).astype(BF16)
    u = _dot(h, wx_ref[...])
    gate_in = _dot(h, wg_ref[...])
    u1, u2, u3 = (pltpu.roll(u, k, 0) for k in (1, 2, 3))
    if paged_prev:
        t = lax.broadcasted_iota(jnp.int32, (tm, D_RNN), 0) & (SUBLANES - 1)
        u1 = jnp.where(t == 0, p1_ref[...], u1)
        u2 = jnp.where(t < 2, p2_ref[...], u2)
        u3 = jnp.where(t < 3, p3_ref[...], u3)
        u_ref[...] = u
    else:
        @pl.when((mi % tiles_per_seq) == 0)
        def _():
            cu_s[...] = jnp.zeros(cu_s.shape, F32)
            ch_s[...] = jnp.zeros(ch_s.shape, F32)

        c = cu_s[...]
        row = lax.broadcasted_iota(jnp.int32, (SUBLANES, D_RNN), 0)
        l1, l2, l3 = (c[SUBLANES - k:SUBLANES - k + 1] for k in (1, 2, 3))
        top1 = jnp.where(row == 0, l1, u1[:SUBLANES])
        top2 = jnp.where(row == 0, l2, jnp.where(row == 1, l1, u2[:SUBLANES]))
        top3 = jnp.where(row == 0, l3, jnp.where(row == 1, l2, jnp.where(row == 2, l1, u3[:SUBLANES])))
        u1, u2, u3 = (jnp.concatenate([top, full[SUBLANES:]], axis=0)
                      for top, full in ((top1, u1), (top2, u2), (top3, u3)))
        cu_s[...] = u[tm - SUBLANES:, :]
        u_ref[0] = u[tm - (LRU_CONV_W - 1):, :]
    cw = cw_ref[...]
    xc = cb_ref[...] + u3 * cw[0:1] + u2 * cw[1:2] + u1 * cw[2:3] + u * cw[3:4]

    xb = xc.astype(BF16)
    tile = 2 * LANES
    nt = D_RNN // tile
    near = lambda ct: [kc for kc in (ct - 1, ct, ct + 1) if 0 <= kc < nt]

    def gate_proj(off):
        tiles = []
        for ct in range(nt):
            cols = slice(off + ct * tile, off + (ct + 1) * tile)
            acc = None
            for kc in near(ct):
                part = _dot(xb[:, kc * tile:(kc + 1) * tile], wri_ref[kc * tile:(kc + 1) * tile, cols])
                acc = part if acc is None else part + acc
            tiles.append(bri_ref[:, cols] + acc)
        return jnp.concatenate(tiles, axis=1)

    rg = _sigmoid(gate_proj(0))
    ig = _sigmoid(gate_proj(D_RNN))
    log_a = rg * (LRU_C * _log_sigmoid(lam_ref[...]))
    a_all = jnp.exp(log_a)
    bx_all = jnp.sqrt(-jnp.tanh(log_a) * (a_all * a_all + 1.0)) * (ig * xc)

    r8 = lax.broadcasted_iota(jnp.int32, (SUBLANES, D_RNN), 0)
    hc = None if paged_prev else ch_s[SUBLANES - 1:SUBLANES, :]
    groups = []
    for gi in range(tm // SUBLANES):
        av = a_all[gi * SUBLANES:(gi + 1) * SUBLANES]
        bv = bx_all[gi * SUBLANES:(gi + 1) * SUBLANES]
        for s in (1, 2, 4):
            keep = r8 >= s
            bv = jnp.where(keep, av * pltpu.roll(bv, s, 0) + bv, bv)
            av = jnp.where(keep, av * pltpu.roll(av, s, 0), av)
        if paged_prev:
            hc = h0_ref[gi:gi + 1, :]
        hg = av * hc + bv
        groups.append(hg)
        hc = hg[SUBLANES - 1:SUBLANES, :]
    hs = jnp.concatenate(groups, axis=0)
    if paged_prev:
        hl_ref[...] = hs
    else:
        ch_s[...] = groups[-1]
        hl_ref[0] = hs[tm - 1:, :]
    mix_ref[...] = (hs * _gelu(gate_in)).astype(BF16)


def _lru(x2d, g, wg, wx, cw, cb, wri, bri, lam, b, t, tm, prev=None):
    m = x2d.shape[0]
    paged_prev = prev is not None
    tiles_per_seq = max(t // tm, 1)
    rows = lambda n: pl.BlockSpec((tm, n), lambda i: (i, 0))
    in_specs = [rows(D_MODEL), _resident((1, D_MODEL)), _resident((D_MODEL, D_RNN)), _resident((D_MODEL, D_RNN)),
                _resident((LRU_CONV_W, D_RNN)), _resident((1, D_RNN)), _resident((D_RNN, 2 * D_RNN)),
                _resident((1, 2 * D_RNN)), _resident((1, D_RNN))]
    args = [x2d, g, wg, wx, cw, cb, wri, bri, lam]
    scratch = []
    if paged_prev:
        in_specs += [rows(D_RNN)] * 3 + [_resident(prev[3].shape)]
        args += list(prev)
        u_spec, hl_spec = rows(D_RNN), rows(D_RNN)
        u_shape = hl_shape = jax.ShapeDtypeStruct((m, D_RNN), F32)
    else:
        u_spec = pl.BlockSpec((1, LRU_CONV_W - 1, D_RNN), lambda i: (i, 0, 0))
        hl_spec = pl.BlockSpec((1, 1, D_RNN), lambda i: (i, 0, 0))
        u_shape = jax.ShapeDtypeStruct((m // tm, LRU_CONV_W - 1, D_RNN), F32)
        hl_shape = jax.ShapeDtypeStruct((m // tm, 1, D_RNN), F32)
        scratch += [pltpu.VMEM((SUBLANES, D_RNN), F32)] * 2
    return pl.pallas_call(
        functools.partial(_lru_kernel, tm=tm, tiles_per_seq=tiles_per_seq, paged_prev=paged_prev),
        grid=(m // tm,),
        in_specs=in_specs,
        out_specs=(rows(D_RNN), u_spec, hl_spec),
        out_shape=(jax.ShapeDtypeStruct((m, D_RNN), BF16), u_shape, hl_shape),
        scratch_shapes=scratch,
        compiler_params=_params(1),
        name="lru",
    )(*args)


def _rope_tables(pos):
    half = ROT_DIMS // 2
    inv = ROPE_THETA ** (-jnp.arange(half, dtype=F32) / half)
    ang = pos.astype(F32)[:, None] * inv[None, :]
    lane = jnp.arange(LANES) % DH_DIFF
    cos = jnp.cos(ang)[:, lane % half]
    sin = jnp.sin(ang)[:, lane % half]
    cos_t = jnp.where(lane[None, :] < ROT_DIMS, cos, 1.0)
    sin_t = jnp.where(lane[None, :] < half, -sin, jnp.where(lane[None, :] < ROT_DIMS, sin, 0.0))
    return cos_t, sin_t


def _block_diag(w):
    n, c, d = w.shape
    idx = jnp.arange(n)
    return jnp.zeros((n, c, n, d), w.dtype).at[idx, :, idx, :].set(w).reshape(n * c, n * d)


def _expand_prev(buf, t, k):
    bs, wm1, c = buf.shape
    out = jnp.zeros((bs, t, c), buf.dtype)
    for step in range(min(k, t)):
        out = out.at[:, step].set(buf[:, wm1 - k + step])
    return out.reshape(bs * t, c)


def kernel(x_prompt, x_sample, cache_fox_k, cache_fox_v, cache_fox_logf, cache_diff_k, cache_diff_v, state_lru_conv, state_lru_h, state_ffn_conv, page_table, mix_norm_g, ab_w_in, ab_b_f, ab_lam_q1, ab_lam_k1, ab_lam_q2, ab_lam_k2, ab_subln_g, ab_w_out, lru_w_gate, lru_w_x, lru_conv_w, lru_conv_b, lru_w_a, lru_b_a, lru_w_i, lru_b_i, lru_lambda, lru_w_out, ffn_norm_g, ffn_w_a, ffn_w_b, ffn_conv_w, ffn_conv_b, ffn_w_down, final_norm_g):
    depth = mix_norm_g.shape[0]
    assert depth == 2 and ab_w_in.shape[0] == 1 and lru_w_x.shape[0] == 1

    w_in = ab_w_in[0]
    o_f = 3 * D_FOX
    w_all = jnp.concatenate([w_in[:, :o_f], w_in[:, o_f + H_FOX:], w_in[:, o_f:o_f + H_FOX],
                             jnp.zeros((D_MODEL, LANES - H_FOX), F32)], axis=1).astype(BF16)
    bf_pad = jnp.pad(ab_b_f[0], (0, LANES - H_FOX)).reshape(1, LANES)
    lamv = jnp.stack([ab_lam_q1[0], ab_lam_k1[0], ab_lam_q2[0], ab_lam_k2[0]]).astype(F32)
    subln_g = ab_subln_g[0].reshape(1, LANES)
    w_out = ab_w_out[0].astype(BF16)
    wg = lru_w_gate[0].astype(BF16)
    wx = lru_w_x[0].astype(BF16)
    wri = jnp.concatenate([_block_diag(lru_w_a[0]), _block_diag(lru_w_i[0])], axis=1).astype(BF16)
    bri = jnp.concatenate([lru_b_a[0], lru_b_i[0]]).reshape(1, 2 * D_RNN)
    lru_wo = lru_w_out[0].astype(BF16)
    ffn_wa = ffn_w_a.astype(BF16)
    ffn_wb = ffn_w_b.astype(BF16)
    ffn_wd = ffn_w_down.astype(BF16)
    lam_init = 0.8 - 0.6 * math.exp(-0.3 * 0)

    def layer0_proj(x2d, pos_rows, tm):
        cos_t, sin_t = _rope_tables(pos_rows)
        return _inproj(x2d, mix_norm_g[0].reshape(1, D_MODEL), w_all, bf_pad, cos_t, sin_t, tm)

    def tail(x2d, mix0, b, t, tm_ffn, tm_lru, tf, ffn_prev, lru_prev):
        x1, nb0 = _ffn(x2d, mix0, w_out, ffn_norm_g[0].reshape(1, D_MODEL), ffn_wa[0], ffn_wb[0],
                       ffn_conv_w[0], ffn_conv_b[0].reshape(1, D_FF), ffn_wd[0], b, t, tm_ffn, tf,
                       prev=ffn_prev[0])
        mix1, u_out, h_out = _lru(x1, mix_norm_g[1].reshape(1, D_MODEL), wg, wx, lru_conv_w[0],
                                  lru_conv_b[0].reshape(1, D_RNN), wri, bri, lru_lambda[0].reshape(1, D_RNN),
                                  b, t, tm_lru, prev=lru_prev)
        y, nb1 = _ffn(x1, (mix1,), lru_wo, ffn_norm_g[1].reshape(1, D_MODEL), ffn_wa[1], ffn_wb[1],
                      ffn_conv_w[1], ffn_conv_b[1].reshape(1, D_FF), ffn_wd[1], b, t, tm_ffn, tf,
                      prev=ffn_prev[1], final_g=final_norm_g.reshape(1, D_MODEL))
        return y, nb0, nb1, u_out, h_out

    bp, tp, _ = x_prompt.shape
    mp = bp * tp
    xp = x_prompt.reshape(mp, D_MODEL)
    (qf, kf, vf, kfb, vfb, logf, logft, qd, kd, vd, kdb, vdb) = layer0_proj(xp, jnp.arange(tp), 512)
    cum = _cumsum_prompt(logft, bp, tp)
    r3 = lambda a: a.reshape(bp, tp, D_FOX)
    tq = 512
    of = _attn_prompt(r3(qf), r3(kfb), r3(vfb), (cum,), True, tq, lam_init)
    od = _attn_prompt(r3(qd), r3(kdb), r3(vdb), (lamv, subln_g), False, tq, lam_init)
    mix0 = (of.reshape(mp, D_FOX), od.reshape(mp, D_DIFF))
    tm_ffn, tm_lru = 256, 256
    y_p, nb0, nb1, u_out, h_out = tail(xp, mix0, bp, tp, tm_ffn, tm_lru, 512, (None, None), None)
    seq_last = lambda a, tm: a[tp // tm - 1::tp // tm]
    nb0, nb1, u_out, h_out = seq_last(nb0, tm_ffn), seq_last(nb1, tm_ffn), seq_last(u_out, tm_lru), seq_last(h_out, tm_lru)
    y_prompt = y_p.reshape(bp, tp, D_MODEL)
    p_state = (kf.reshape(1, bp, tp, H_FOX, DH_FOX), vf.reshape(1, bp, tp, H_FOX, DH_FOX),
               logf.reshape(1, bp, tp, H_FOX),
               kd.reshape(1, bp, tp, H_DIFF, 2 * DH_DIFF), vd.reshape(1, bp, tp, H_DIFF, 2 * DH_DIFF),
               u_out[None], h_out.reshape(1, bp, D_RNN), jnp.stack([nb0, nb1]))

    bs, ts, _ = x_sample.shape
    assert ts == SUBLANES
    ms = bs * ts
    n_pages = page_table.shape[1]
    past = n_pages * PAGE_SIZE
    xs = x_sample.reshape(ms, D_MODEL)
    pos_rows = jnp.tile(past + jnp.arange(ts), bs)
    (qf, kf, vf, kfb, vfb, logf, logft, qd, kd, vd, kdb, vdb) = layer0_proj(xs, pos_rows, ms)
    n_pool = cache_fox_k.shape[1]
    logf_pool = jnp.swapaxes(cache_fox_logf[0], 1, 2).reshape(n_pool * H_FOX, PAGE_SIZE)
    lc = _cumsum_rows(logf_pool).reshape(n_pool, H_FOX, PAGE_SIZE)
    logfn = jnp.pad(logft.reshape(H_FOX, bs, ts).transpose(1, 0, 2), ((0, 0), (0, 0), (0, LANES - ts)))
    pad_rows = lambda a: jnp.pad(a.reshape(bs, ts, D_FOX), ((0, 0), (0, PAGE_SIZE - ts), (0, 0)))
    kt_pages = lambda c: jnp.transpose(c, (0, 2, 3, 1)).reshape(n_pool, D_FOX, PAGE_SIZE)
    mix0 = _attn_decode(page_table, qf.astype(F32).reshape(bs, ts, D_FOX), qd.astype(F32).reshape(bs, ts, D_DIFF),
                        pad_rows(kfb), pad_rows(vfb), pad_rows(kdb), pad_rows(vdb),
                        kt_pages(cache_fox_k[0]), kt_pages(cache_fox_v[0]), lc,
                        cache_diff_k[0].reshape(n_pool, PAGE_SIZE * H_DIFF, 2 * DH_DIFF),
                        cache_diff_v[0].reshape(n_pool, PAGE_SIZE * H_DIFF, 2 * DH_DIFF),
                        logfn, lamv, subln_g, 16, lam_init).reshape(ms, D_MIX_AB)
    ffn_prev = tuple((_expand_prev(state_ffn_conv[l], ts, 1), _expand_prev(state_ffn_conv[l], ts, 2))
                     for l in range(depth))
    lru_prev = tuple(_expand_prev(state_lru_conv[0], ts, k) for k in (1, 2, 3)) + (state_lru_h[0],)
    y_s, a0, a1, u_full, h_full = tail(xs, (mix0,), bs, ts, ms, ms, 512, ffn_prev, lru_prev)
    y_sample = y_s.reshape(bs, ts, D_MODEL)
    last = lambda a, k: a.reshape(bs, ts, -1)[:, ts - k:]
    s_state = (kf.reshape(1, bs, ts, H_FOX, DH_FOX), vf.reshape(1, bs, ts, H_FOX, DH_FOX),
               logf.reshape(1, bs, ts, H_FOX),
               kd.reshape(1, bs, ts, H_DIFF, 2 * DH_DIFF), vd.reshape(1, bs, ts, H_DIFF, 2 * DH_DIFF),
               last(u_full, LRU_CONV_W - 1)[None], last(h_full, 1).reshape(1, bs, D_RNN),
               jnp.stack([last(a0, FFN_CONV_W - 1), last(a1, FFN_CONV_W - 1)]))

    return (y_prompt, y_sample) + p_state + s_state
```

```python
import functools
import math

import jax
import jax.numpy as jnp
from jax import lax
from jax.experimental import pallas as pl
from jax.experimental.pallas import tpu as pltpu

F32 = jnp.float32
BF16 = jnp.bfloat16

D_MODEL = 1024
H_FOX = 8
DH_FOX = 64
H_DIFF = 4
DH_DIFF = 64
D_FOX = H_FOX * DH_FOX
D_DIFF = H_DIFF * 2 * DH_DIFF
D_MIX_AB = D_FOX + D_DIFF
ROT_DIMS = DH_DIFF // 4
ROPE_THETA = 500000.0
D_RNN = 1280
N_LRU_BLOCKS = 16
LRU_BW = D_RNN // N_LRU_BLOCKS
LRU_CONV_W = 4
LRU_C = 8.0
D_FF = 3072
FFN_CONV_W = 3
EPS = 1e-6
NEG = -1e30
PAGE_SIZE = 128
LOG2E = math.log2(math.e)

LANES = 128
SUBLANES = 8
VMEM_LIMIT = 56 * 1024 * 1024
N_QKV = 6 * D_FOX
N_INPROJ = N_QKV + LANES


def _params(n_axes):
    return pltpu.CompilerParams(dimension_semantics=("arbitrary",) * n_axes,
                                vmem_limit_bytes=VMEM_LIMIT)


def _full(shape):
    n = len(shape)
    return pl.BlockSpec(shape, lambda *_: (0,) * n)


def _resident(shape):
    n = len(shape)
    return pl.BlockSpec(shape, lambda *_: (0,) * n, pipeline_mode=pl.Buffered(1))


def _rms(x, g):
    return x * lax.rsqrt(jnp.mean(x * x, axis=-1, keepdims=True) + EPS) * g


def _gelu(x):
    c = math.sqrt(2.0 / math.pi)
    return x * (0.5 * (1.0 + jnp.tanh(c * (x + 0.044715 * (x * x * x)))))


def _log_sigmoid(z):
    return -(jnp.maximum(-z, 0.0) + jnp.log1p(jnp.exp(-jnp.abs(z))))


def _sigmoid(z):
    return 1.0 / (1.0 + jnp.exp(-z))


def _dot(a, b):
    return jnp.dot(a, b, preferred_element_type=F32)


def _dot_nt(a, b):
    return lax.dot_general(a, b, (((1,), (1,)), ((), ())), preferred_element_type=F32)


def _prefix_lanes(x):
    r = lax.broadcasted_iota(jnp.int32, (LANES, LANES), 0)
    c = lax.broadcasted_iota(jnp.int32, (LANES, LANES), 1)
    tri = jnp.where(r <= c, 1.0, 0.0).astype(BF16)
    hi = x.astype(BF16)
    r1 = x - hi.astype(F32)
    mid = r1.astype(BF16)
    lo = (r1 - mid.astype(F32)).astype(BF16)
    return _dot(hi, tri) + _dot(mid, tri) + _dot(lo, tri)


def _inproj_kernel(x_ref, g_ref, w_ref, bf_ref, cos_ref, sin_ref,
                   qf_ref, kf_ref, vf_ref, kfb_ref, vfb_ref, logf_ref, logft_ref,
                   qd_ref, kd_ref, vd_ref, kdb_ref, vdb_ref):
    h = _rms(x_ref[...], g_ref[...]).astype(BF16)

    def proj(i, width=D_FOX):
        return _dot(h, w_ref[:, i * D_FOX:i * D_FOX + width])

    qf_ref[...] = (proj(0) * (DH_FOX ** -0.5 * LOG2E)).astype(BF16)
    def store_state(o_ref, x):
        if len(o_ref.shape) == 3:
            o_ref[0] = x.T
        else:
            o_ref[...] = x

    kf = proj(1)
    store_state(kf_ref, kf)
    kfb_ref[...] = kf.astype(BF16)
    vf = proj(2)
    store_state(vf_ref, vf)
    vfb_ref[...] = vf.astype(BF16)

    logf = _log_sigmoid(proj(6, LANES) + bf_ref[...])
    logf_ref[...] = logf[:, :H_FOX]
    logft_ref[...] = logf.T[:H_FOX, :]

    cosf = jnp.concatenate([cos_ref[...]] * (D_DIFF // LANES), axis=1)
    sinf = jnp.concatenate([sin_ref[...]] * (D_DIFF // LANES), axis=1)
    lane = lax.broadcasted_iota(jnp.int32, cosf.shape, 1) & (DH_DIFF - 1)
    half = ROT_DIMS // 2

    def rope(x):
        partner = jnp.where(lane < half, pltpu.roll(x, D_DIFF - half, 1), pltpu.roll(x, half, 1))
        return jnp.where(lane < ROT_DIMS, x * cosf + partner * sinf, x)

    def store_heads(o_ref, x):
        for hd in range(H_DIFF):
            o_ref[pl.ds(hd, x.shape[0], stride=H_DIFF), :] = x[:, hd * LANES:(hd + 1) * LANES]

    qd_ref[...] = (rope(proj(3)) * (DH_DIFF ** -0.5 * LOG2E)).astype(BF16)
    kd = rope(proj(4))
    store_heads(kd_ref, kd)
    kdb_ref[...] = kd.astype(BF16)
    vd = proj(5)
    store_heads(vd_ref, vd)
    vdb_ref[...] = vd.astype(BF16)


def _inproj(x2d, g, w_all, bf_pad, cos_t, sin_t, tm, seq_len=None):
    m = x2d.shape[0]
    nt = cos_t.shape[0] // tm
    row = lambda n: pl.BlockSpec((tm, n), lambda i: (i, 0))
    tab = pl.BlockSpec((tm, LANES), lambda i: (i % nt, 0))
    b16o = jax.ShapeDtypeStruct((m, D_FOX), BF16)
    if seq_len is None:
        fox_o, fox_spec = jax.ShapeDtypeStruct((m, D_FOX), F32), row(D_FOX)
    else:
        npt = seq_len // tm
        fox_o = jax.ShapeDtypeStruct((m // seq_len, D_FOX, seq_len), F32)
        fox_spec = pl.BlockSpec((1, D_FOX, tm), lambda i: (i // npt, 0, i % npt))
    heads_o = jax.ShapeDtypeStruct((m * H_DIFF, LANES), F32)
    heads_spec = pl.BlockSpec((tm * H_DIFF, LANES), lambda i: (i, 0))
    out_shape = (b16o, fox_o, fox_o, b16o, b16o,
                 jax.ShapeDtypeStruct((m, H_FOX), F32), jax.ShapeDtypeStruct((H_FOX, m), F32),
                 b16o, heads_o, heads_o, b16o, b16o)
    out_specs = ((row(D_FOX), fox_spec, fox_spec, row(D_FOX), row(D_FOX))
                 + (row(H_FOX), pl.BlockSpec((H_FOX, tm), lambda i: (0, i)))
                 + (row(D_FOX), heads_spec, heads_spec, row(D_FOX), row(D_FOX)))
    return pl.pallas_call(
        _inproj_kernel,
        grid=(m // tm,),
        in_specs=[row(D_MODEL), _full((1, D_MODEL)), _full((D_MODEL, N_INPROJ)), _full((1, LANES)), tab, tab],
        out_specs=out_specs,
        out_shape=out_shape,
        compiler_params=_params(1),
        name="inproj",
    )(x2d, g, w_all, bf_pad, cos_t, sin_t)


def _cumsum_prompt_kernel(x_ref, o_ref, *, t):
    carry = jnp.zeros((H_FOX, 1), F32)
    for i in range(t // LANES):
        c = _prefix_lanes(x_ref[:, i * LANES:(i + 1) * LANES]) + carry
        c2 = c * LOG2E
        for j in range(H_FOX // 2):
            o_ref[0, j, :, i * LANES:(i + 1) * LANES] = c2[2 * j:2 * j + 2, :]
        carry = c[:, LANES - 1:LANES]


def _cumsum_prompt(logft, b, t):
    return pl.pallas_call(
        functools.partial(_cumsum_prompt_kernel, t=t),
        grid=(b,),
        in_specs=[pl.BlockSpec((H_FOX, t), lambda i: (0, i))],
        out_specs=pl.BlockSpec((1, H_FOX // 2, 2, t), lambda i: (i, 0, 0, 0)),
        out_shape=jax.ShapeDtypeStruct((b, H_FOX // 2, 2, t), F32),
        compiler_params=_params(1),
        name="cumsum_prompt",
    )(logft)


def _cumsum_rows_kernel(x_ref, o_ref):
    o_ref[...] = _prefix_lanes(x_ref[...])


def _cumsum_rows(x, max_rows=2048):
    rows = x.shape[0]
    tr = max(d for d in range(SUBLANES, max_rows + 1, SUBLANES) if rows % d == 0)
    spec = pl.BlockSpec((tr, LANES), lambda i: (i, 0))
    return pl.pallas_call(
        _cumsum_rows_kernel,
        grid=(rows // tr,),
        in_specs=[spec],
        out_specs=spec,
        out_shape=jax.ShapeDtypeStruct((rows, LANES), F32),
        compiler_params=_params(1),
        name="cumsum_pages",
    )(x)


def _stack_masked(q):
    lane = lax.broadcasted_iota(jnp.int32, q.shape, 1)
    zero = jnp.zeros_like(q)
    return jnp.concatenate([jnp.where(lane < LANES // 2, q, zero),
                            jnp.where(lane >= LANES // 2, q, zero)], axis=0)


def _lambda(lamv, lam_init):
    s1 = jnp.sum(lamv[0:1] * lamv[1:2], axis=1, keepdims=True)
    s2 = jnp.sum(lamv[2:3] * lamv[3:4], axis=1, keepdims=True)
    return jnp.exp(s1) - jnp.exp(s2) + lam_init


def _fox_out(acc, l, r):
    o = acc / l
    lane = lax.broadcasted_iota(jnp.int32, (r, LANES), 1)
    return jnp.where(lane < LANES // 2, o[:r], o[r:])


def _diff_out(acc, l, r, lamv, g, lam_init):
    o = acc / l
    od = o[:r] - _lambda(lamv, lam_init) * o[r:]
    return _rms(od, g) * (1.0 - lam_init)


def _attn_prompt_kernel(*refs, is_fox, tq, rc, lam_init):
    if is_fox:
        q_ref, k_ref, v_ref, c_ref, o_ref = refs[:5]
    else:
        q_ref, k_ref, v_ref, lamv_ref, g_ref, o_ref = refs[:6]
    s0_s, s1_s, r0_s, r1_s, m_s, l_s, acc_s = refs[-7:]
    qi = pl.program_id(2)
    q2 = _stack_masked(q_ref[0])
    nrep = tq // LANES

    def keys(kb):
        return pl.ds(pl.multiple_of(kb * tq, tq), tq)

    def row_max(s):
        return jnp.broadcast_to(jnp.max(s, axis=1, keepdims=True), (2 * tq, LANES))

    def qk(kb, s_s, r_s):
        s = _dot_nt(q2, k_ref[0, keys(kb), :])
        if is_fox:
            c0 = c_ref[0, 0, 0:1, keys(kb)]
            c1 = c_ref[0, 0, 1:2, keys(kb)]
            s = jnp.concatenate([s[:tq] - c0, s[tq:] - c1], axis=0)
        s_s[...] = s
        r_s[...] = row_max(s)

    def softmax_pv(kb, s_s, r_s):
        m_old = m_s[...]
        m_new = jnp.maximum(m_old, r_s[...])
        alpha = jnp.exp2(m_old - m_new)
        m_s[...] = m_new
        sums, probs = [], []
        for ci in range(2 * tq // rc):
            rows = slice(ci * rc, (ci + 1) * rc)
            p = jnp.exp2(s_s[rows, :] - jnp.tile(m_new[rows], (1, nrep)))
            sums.append(sum(p[:, i * LANES:(i + 1) * LANES] for i in range(nrep)))
            probs.append(p.astype(BF16))
        l_s[...] = alpha * l_s[...] + jnp.concatenate(sums, axis=0)
        acc_s[...] = alpha * acc_s[...] + _dot(jnp.concatenate(probs, axis=0), v_ref[0, keys(kb), :])

    m_s[...] = jnp.full(m_s.shape, NEG, F32)
    l_s[...] = jnp.zeros(l_s.shape, F32)
    acc_s[...] = jnp.zeros(acc_s.shape, F32)
    bufs = ((s0_s, r0_s), (s1_s, r1_s))
    qk(0, *bufs[0])

    def body(kb, _):
        for parity in range(2):
            @pl.when((kb & 1) == parity)
            def _():
                qk(kb + 1, *bufs[1 - parity])
                softmax_pv(kb, *bufs[parity])
        return 0

    lax.fori_loop(0, qi, body, 0)
    row = lax.broadcasted_iota(jnp.int32, (2 * tq, tq), 0)
    col = lax.broadcasted_iota(jnp.int32, (2 * tq, tq), 1)
    causal = col <= jnp.where(row >= tq, row - tq, row)
    for parity in range(2):
        @pl.when((qi & 1) == parity)
        def _():
            s_s, r_s = bufs[parity]
            s = jnp.where(causal, s_s[...], NEG)
            s_s[...] = s
            r_s[...] = row_max(s)
            softmax_pv(qi, s_s, r_s)

    acc = acc_s[...]
    l = jnp.sum(l_s[...], axis=1, keepdims=True)
    if is_fox:
        o_ref[0] = _fox_out(acc, l, tq).astype(BF16)
    else:
        o_ref[0] = _diff_out(acc, l, tq, lamv_ref[...], g_ref[...], lam_init).astype(BF16)


def _attn_prompt(q, k, v, extra, is_fox, tq, lam_init, rc=32):
    b, t, _ = q.shape
    nblk = D_FOX // LANES
    wide = pltpu.VMEM((2 * tq, LANES), F32)
    scratch = [pltpu.VMEM((2 * tq, tq), F32), pltpu.VMEM((2 * tq, tq), F32), wide, wide, wide, wide, wide]
    qspec = pl.BlockSpec((1, tq, LANES), lambda bi, j, qi: (bi, qi, j))
    kvspec = pl.BlockSpec((1, t, LANES), lambda bi, j, qi: (bi, 0, j))
    if is_fox:
        especs = [pl.BlockSpec((1, 1, 2, t), lambda bi, j, qi: (bi, j, 0, 0))]
    else:
        especs = [_full((4, DH_DIFF)), _full((1, LANES))]
    return pl.pallas_call(
        functools.partial(_attn_prompt_kernel, is_fox=is_fox, tq=tq, rc=rc, lam_init=lam_init),
        grid=(b, nblk, t // tq),
        in_specs=[qspec, kvspec, kvspec] + especs,
        out_specs=qspec,
        out_shape=jax.ShapeDtypeStruct((b, t, D_FOX), BF16),
        scratch_shapes=scratch,
        compiler_params=_params(3),
        name="attn_prompt_fox" if is_fox else "attn_prompt_diff",
    )(q, k, v, *extra)


def _attn_decode_kernel(pt_ref, *refs, g_pages, n_steps, t_new, lam_init):
    n_in = 6 + 5 * g_pages + 5
    (qf_ref, qd_ref, kfn_ref, vfn_ref, kdn_ref, vdn_ref) = refs[:6]
    pages = refs[6:6 + 5 * g_pages]
    fk_refs, fv_refs, lc_refs, dk_refs, dv_refs = (pages[i * g_pages:(i + 1) * g_pages] for i in range(5))
    logfn_ref, lamv_ref, g_ref = refs[6 + 5 * g_pages:6 + 5 * g_pages + 3]
    o_ref = refs[n_in - 2]
    q2_s, m_s, l_s, acc_s, cc_s = refs[n_in - 1:]
    del pt_ref
    p = pl.program_id(1)
    nfb = D_FOX // LANES
    ndb = D_DIFF // LANES
    r = SUBLANES

    nblk = nfb + ndb
    rb = 2 * r

    def blk(x, j):
        return x[j * rb:(j + 1) * rb]

    def lanes(x, j):
        return x[:, j * LANES:(j + 1) * LANES]

    @pl.when(p == 0)
    def _():
        q = jnp.concatenate([qf_ref[0], qd_ref[0]], axis=1)
        q2_s[...] = jnp.concatenate([_stack_masked(lanes(q, j)) for j in range(nblk)], axis=0).astype(BF16)
        m_s[...] = jnp.full(m_s.shape, NEG, F32)
        l_s[...] = jnp.zeros(l_s.shape, F32)
        acc_s[...] = jnp.zeros(acc_s.shape, F32)
        cc_s[...] = jnp.zeros(cc_s.shape, F32)

    def fox_bias(ck):
        ck = ck * LOG2E
        return jnp.concatenate([jnp.broadcast_to(ck[h:h + 1], (r, ck.shape[1])) for h in range(H_FOX)], axis=0)

    nf = nfb * rb
    nd = ndb * rb

    def update(lo, n, u, pv_fn):
        sl = slice(lo, lo + n)
        m = m_s[sl]
        m_new = jnp.maximum(m, jnp.max(u, axis=1, keepdims=True))
        alpha = jnp.exp2(m - m_new)
        pe = jnp.exp2(u - m_new)
        m_s[sl] = m_new
        l_s[sl] = alpha * l_s[sl] + jnp.sum(pe, axis=1, keepdims=True)
        acc_s[sl] = alpha * acc_s[sl] + pv_fn(pe.astype(BF16))

    @pl.when(p < n_steps)
    def _():
        cks = []
        run = cc_s[...]
        for gi in range(g_pages):
            lc = lc_refs[gi][0]
            cks.append(run + lc)
            run = run + lc[:, LANES - 1:LANES]
        cc_s[...] = run
        q2 = q2_s[...]

        def fkv(refs, gi, j):
            return refs[gi][0, j * LANES:(j + 1) * LANES, :].astype(BF16)

        s_fox = jnp.concatenate(
            [jnp.concatenate([_dot(blk(q2, j), fkv(fk_refs, gi, j)) for gi in range(g_pages)], axis=1)
             for j in range(nfb)], axis=0)

        def pv_fox(pb):
            return jnp.concatenate(
                [sum(_dot_nt(lanes(blk(pb, j), gi), fkv(fv_refs, gi, j)) for gi in range(g_pages))
                 for j in range(nfb)], axis=0)

        update(0, nf, s_fox - fox_bias(jnp.concatenate(cks, axis=1)), pv_fox)

        wd = PAGE_SIZE * H_DIFF
        s_diff = jnp.concatenate([_dot_nt(q2[nf:], dk_refs[gi][0].astype(BF16)) for gi in range(g_pages)], axis=1)
        row = lax.broadcasted_iota(jnp.int32, s_diff.shape, 0)
        col = lax.broadcasted_iota(jnp.int32, s_diff.shape, 1)
        own = (col & (H_DIFF - 1)) == row // rb

        def pv_diff(pb):
            return sum(_dot(pb[:, gi * wd:(gi + 1) * wd], dv_refs[gi][0].astype(BF16)) for gi in range(g_pages))

        update(nf, nd, jnp.where(own, s_diff, NEG), pv_diff)

    @pl.when(p == n_steps)
    def _():
        row = lax.broadcasted_iota(jnp.int32, (nf, LANES), 0) & (r - 1)
        col = lax.broadcasted_iota(jnp.int32, (nf, LANES), 1)
        keep = (col <= row) & (col < t_new)
        ck = cc_s[...] + _prefix_lanes(logfn_ref[0])
        q2 = q2_s[...]
        s_fox = jnp.concatenate([_dot_nt(blk(q2, j), lanes(kfn_ref[0], j)) for j in range(nfb)], axis=0)
        update(0, nf, jnp.where(keep, s_fox - fox_bias(ck), NEG),
               lambda pb: jnp.concatenate([_dot(blk(pb, j), lanes(vfn_ref[0], j)) for j in range(nfb)], axis=0))
        s_diff = jnp.concatenate([_dot_nt(blk(q2, nfb + j), lanes(kdn_ref[0], j)) for j in range(ndb)], axis=0)
        update(nf, nd, jnp.where(keep, s_diff, NEG),
               lambda pb: jnp.concatenate([_dot(blk(pb, j), lanes(vdn_ref[0], j)) for j in range(ndb)], axis=0))
        acc = acc_s[...]
        l = l_s[...]
        lamv = lamv_ref[...]
        for j in range(nfb):
            o_ref[0, :, j * LANES:(j + 1) * LANES] = _fox_out(blk(acc, j), blk(l, j), r).astype(BF16)
        for j in range(nfb, nblk):
            o = _diff_out(blk(acc, j), blk(l, j), r, lamv, g_ref[...], lam_init)
            o_ref[0, :, j * LANES:(j + 1) * LANES] = o.astype(BF16)


def _attn_decode(page_table, qf, qd, kfn, vfn, kdn, vdn, fk, fv, lc, dk, dv, logfn, lamv, g, g_pages, lam_init):
    bs, t_new, _ = qf.shape
    n_pages = page_table.shape[1]
    n_steps = n_pages // g_pages
    nblk = (D_FOX + D_DIFF) // LANES

    def seq(shape):
        return pl.BlockSpec((1,) + shape, lambda b, p, pt: (b, 0, 0))

    def paged(shape, gi):
        def index(b, p, pt):
            return (pt[b, jnp.minimum(p, n_steps - 1) * g_pages + gi],) + (0,) * len(shape)
        return pl.BlockSpec((1,) + shape, index)

    in_specs = [seq((t_new, D_FOX)), seq((t_new, D_DIFF))] + [seq((PAGE_SIZE, D_FOX))] * 4
    args = [qf, qd, kfn, vfn, kdn, vdn]
    for arr in (fk, fv, lc, dk, dv):
        for gi in range(g_pages):
            in_specs.append(paged(arr.shape[1:], gi))
            args.append(arr)
    in_specs += [seq((H_FOX, LANES)),
                 pl.BlockSpec((4, DH_DIFF), lambda b, p, pt: (0, 0)),
                 pl.BlockSpec((1, LANES), lambda b, p, pt: (0, 0))]
    args += [logfn, lamv, g]
    grid_spec = pltpu.PrefetchScalarGridSpec(
        num_scalar_prefetch=1,
        grid=(bs, n_steps + 1),
        in_specs=in_specs,
        out_specs=seq((t_new, D_MIX_AB)),
        scratch_shapes=[pltpu.VMEM((nblk * 2 * t_new, LANES), BF16),
                        pltpu.VMEM((nblk * 2 * t_new, 1), F32),
                        pltpu.VMEM((nblk * 2 * t_new, 1), F32),
                        pltpu.VMEM((nblk * 2 * t_new, LANES), F32),
                        pltpu.VMEM((H_FOX, LANES), F32)],
    )
    return pl.pallas_call(
        functools.partial(_attn_decode_kernel, g_pages=g_pages, n_steps=n_steps, t_new=t_new, lam_init=lam_init),
        grid_spec=grid_spec,
        out_shape=jax.ShapeDtypeStruct((bs, t_new, D_MIX_AB), BF16),
        compiler_params=_params(2),
        name="attn_decode",
    )(page_table, *args)


def _ffn_kernel(*refs, tm, tf, n_mix, tiles_per_seq, paged_prev, final_norm):
    it = iter(refs)
    x_ref = next(it)
    mix_refs = [next(it) for _ in range(n_mix)]
    wo_ref, g_ref, wa_ref, wb_ref, cw_ref, cb_ref, wd_ref = (next(it) for _ in range(7))
    if paged_prev:
        p1_ref, p2_ref = next(it), next(it)
    if final_norm:
        fg_ref = next(it)
    o_ref, nb_ref = next(it), next(it)
    if not paged_prev:
        carry_s = next(it)

        @pl.when((pl.program_id(0) % tiles_per_seq) == 0)
        def _():
            carry_s[...] = jnp.zeros(carry_s.shape, F32)

    y = None
    k0 = 0
    for mix_ref in mix_refs:
        k1 = k0 + mix_ref.shape[1]
        part = _dot(mix_ref[...], wo_ref[k0:k1, :])
        y = part if y is None else part + y
        k0 = k1
    x1 = x_ref[...] + y
    h = _rms(x1, g_ref[...]).astype(BF16)
    nf = D_FF // tf
    row = lax.broadcasted_iota(jnp.int32, (tm if paged_prev else SUBLANES, tf), 0)

    def up(f):
        cols = slice(f * tf, (f + 1) * tf)
        return _dot(h, wa_ref[:, cols]), _dot(h, wb_ref[:, cols])

    def act(f, a, gate):
        cols = slice(f * tf, (f + 1) * tf)
        a1 = pltpu.roll(a, 1, 0)
        a2 = pltpu.roll(a, 2, 0)
        if paged_prev:
            t = row & (SUBLANES - 1)
            a1 = jnp.where(t == 0, p1_ref[:, cols], a1)
            a2 = jnp.where(t < 2, p2_ref[:, cols], a2)
            nb_ref[:, cols] = a
        else:
            c = carry_s[:, cols]
            top1 = jnp.where(row == 0, c[SUBLANES - 1:], a1[:SUBLANES])
            top2 = jnp.where(row == 0, c[SUBLANES - 2:SUBLANES - 1],
                             jnp.where(row == 1, c[SUBLANES - 1:], a2[:SUBLANES]))
            a1 = jnp.concatenate([top1, a1[SUBLANES:]], axis=0)
            a2 = jnp.concatenate([top2, a2[SUBLANES:]], axis=0)
            carry_s[:, cols] = a[tm - SUBLANES:, :]
            nb_ref[0, :, cols] = a[tm - (FFN_CONV_W - 1):, :]
        cw = cw_ref[:, cols]
        ac = cb_ref[:, cols] + a2 * cw[0:1] + a1 * cw[1:2] + a * cw[2:3]
        return (_gelu(ac) * gate).astype(BF16)

    acc = x1
    pending = up(0)
    for f in range(nf):
        following = up(f + 1) if f + 1 < nf else None
        acc = _dot(act(f, *pending), wd_ref[f * tf:(f + 1) * tf, :]) + acc
        pending = following
    if final_norm:
        acc = _rms(acc, fg_ref[...])
    o_ref[...] = acc


def _ffn(x2d, mix_parts, wo, g, wa, wb, cw, cb, wd, b, t, tm, tf, prev=None, final_g=None):
    m = x2d.shape[0]
    paged_prev = prev is not None
    tiles_per_seq = max(t // tm, 1)
    rows = lambda n: pl.BlockSpec((tm, n), lambda i: (i, 0))
    in_specs = ([rows(D_MODEL)] + [rows(p.shape[1]) for p in mix_parts]
                + [_resident(wo.shape), _resident((1, D_MODEL)),
                   _resident((D_MODEL, D_FF)), _resident((D_MODEL, D_FF)), _resident((FFN_CONV_W, D_FF)),
                   _resident((1, D_FF)), _resident((D_FF, D_MODEL))])
    args = [x2d, *mix_parts, wo, g, wa, wb, cw, cb, wd]
    scratch = []
    if paged_prev:
        in_specs += [rows(D_FF)] * 2
        args += list(prev)
        nb_spec = rows(D_FF)
        nb_shape = jax.ShapeDtypeStruct((m, D_FF), F32)
    else:
        nb_spec = pl.BlockSpec((1, FFN_CONV_W - 1, D_FF), lambda i: (i, 0, 0))
        nb_shape = jax.ShapeDtypeStruct((m // tm, FFN_CONV_W - 1, D_FF), F32)
        scratch.append(pltpu.VMEM((SUBLANES, D_FF), F32))
    if final_g is not None:
        in_specs.append(_resident((1, D_MODEL)))
        args.append(final_g)
    return pl.pallas_call(
        functools.partial(_ffn_kernel, tm=tm, tf=tf, n_mix=len(mix_parts), tiles_per_seq=tiles_per_seq,
                          paged_prev=paged_prev, final_norm=final_g is not None),
        grid=(m // tm,),
        in_specs=in_specs,
        out_specs=(rows(D_MODEL), nb_spec),
        out_shape=(jax.ShapeDtypeStruct((m, D_MODEL), F32), nb_shape),
        scratch_shapes=scratch,
        compiler_params=_params(1),
        name="ffn",
    )(*args)


def _lru_kernel(*refs, tm, tiles_per_seq, paged_prev):
    it = iter(refs)
    x_ref, g_ref, wg_ref, wx_ref, cw_ref, cb_ref, wri_ref, bri_ref, lam_ref = (next(it) for _ in range(9))
    if paged_prev:
        p1_ref, p2_ref, p3_ref, h0_ref = (next(it) for _ in range(4))
    mix_ref, u_ref, hl_ref = next(it), next(it), next(it)
    if not paged_prev:
        cu_s, ch_s = next(it), next(it)
    mi = pl.program_id(0)

    h = _rms(x_ref[...], g_ref[...]).astype(BF16)
    u = _dot(h, wx_ref[...])
    gate_in = _dot(h, wg_ref[...])
    u1, u2, u3 = (pltpu.roll(u, k, 0) for k in (1, 2, 3))
    if paged_prev:
        t = lax.broadcasted_iota(jnp.int32, (tm, D_RNN), 0) & (SUBLANES - 1)
        u1 = jnp.where(t == 0, p1_ref[...], u1)
        u2 = jnp.where(t < 2, p2_ref[...], u2)
        u3 = jnp.where(t < 3, p3_ref[...], u3)
        u_ref[...] = u
    else:
        @pl.when((mi % tiles_per_seq) == 0)
        def _():
            cu_s[...] = jnp.zeros(cu_s.shape, F32)
            ch_s[...] = jnp.zeros(ch_s.shape, F32)

        c = cu_s[...]
        row = lax.broadcasted_iota(jnp.int32, (SUBLANES, D_RNN), 0)
        l1, l2, l3 = (c[SUBLANES - k:SUBLANES - k + 1] for k in (1, 2, 3))
        top1 = jnp.where(row == 0, l1, u1[:SUBLANES])
        top2 = jnp.where(row == 0, l2, jnp.where(row == 1, l1, u2[:SUBLANES]))
        top3 = jnp.where(row == 0, l3, jnp.where(row == 1, l2, jnp.where(row == 2, l1, u3[:SUBLANES])))
        u1, u2, u3 = (jnp.concatenate([top, full[SUBLANES:]], axis=0)
                      for top, full in ((top1, u1), (top2, u2), (top3, u3)))
        cu_s[...] = u[tm - SUBLANES:, :]
        u_ref[0] = u[tm - (LRU_CONV_W - 1):, :]
    cw = cw_ref[...]
    xc = cb_ref[...] + u3 * cw[0:1] + u2 * cw[1:2] + u1 * cw[2:3] + u * cw[3:4]

    xb = xc.astype(BF16)
    tile = 2 * LANES
    nt = D_RNN // tile
    near = lambda ct: [kc for kc in (ct - 1, ct, ct + 1) if 0 <= kc < nt]

    def gate_proj(off):
        tiles = []
        for ct in range(nt):
            cols = slice(off + ct * tile, off + (ct + 1) * tile)
            acc = None
            for kc in near(ct):
                part = _dot(xb[:, kc * tile:(kc + 1) * tile], wri_ref[kc * tile:(kc + 1) * tile, cols])
                acc = part if acc is None else part + acc
            tiles.append(bri_ref[:, cols] + acc)
        return jnp.concatenate(tiles, axis=1)

    rg = _sigmoid(gate_proj(0))
    ig = _sigmoid(gate_proj(D_RNN))
    log_a = rg * (LRU_C * _log_sigmoid(lam_ref[...]))
    a_all = jnp.exp(log_a)
    bx_all = jnp.sqrt(-jnp.tanh(log_a) * (a_all * a_all + 1.0)) * (ig * xc)

    r8 = lax.broadcasted_iota(jnp.int32, (SUBLANES, D_RNN), 0)
    hc = None if paged_prev else ch_s[SUBLANES - 1:SUBLANES, :]
    groups = []
    for gi in range(tm // SUBLANES):
        av = a_all[gi * SUBLANES:(gi + 1) * SUBLANES]
        bv = bx_all[gi * SUBLANES:(gi + 1) * SUBLANES]
        for s in (1, 2, 4):
            keep = r8 >= s
            bv = jnp.where(keep, av * pltpu.roll(bv, s, 0) + bv, bv)
            av = jnp.where(keep, av * pltpu.roll(av, s, 0), av)
        if paged_prev:
            hc = h0_ref[gi:gi + 1, :]
        hg = av * hc + bv
        groups.append(hg)
        hc = hg[SUBLANES - 1:SUBLANES, :]
    hs = jnp.concatenate(groups, axis=0)
    if paged_prev:
        hl_ref[...] = hs
    else:
        ch_s[...] = groups[-1]
        hl_ref[0] = hs[tm - 1:, :]
    mix_ref[...] = (hs * _gelu(gate_in)).astype(BF16)


def _lru(x2d, g, wg, wx, cw, cb, wri, bri, lam, b, t, tm, prev=None):
    m = x2d.shape[0]
    paged_prev = prev is not None
    tiles_per_seq = max(t // tm, 1)
    rows = lambda n: pl.BlockSpec((tm, n), lambda i: (i, 0))
    in_specs = [rows(D_MODEL), _resident((1, D_MODEL)), _resident((D_MODEL, D_RNN)), _resident((D_MODEL, D_RNN)),
                _resident((LRU_CONV_W, D_RNN)), _resident((1, D_RNN)), _resident((D_RNN, 2 * D_RNN)),
                _resident((1, 2 * D_RNN)), _resident((1, D_RNN))]
    args = [x2d, g, wg, wx, cw, cb, wri, bri, lam]
    scratch = []
    if paged_prev:
        in_specs += [rows(D_RNN)] * 3 + [_resident(prev[3].shape)]
        args += list(prev)
        u_spec, hl_spec = rows(D_RNN), rows(D_RNN)
        u_shape = hl_shape = jax.ShapeDtypeStruct((m, D_RNN), F32)
    else:
        u_spec = pl.BlockSpec((1, LRU_CONV_W - 1, D_RNN), lambda i: (i, 0, 0))
        hl_spec = pl.BlockSpec((1, 1, D_RNN), lambda i: (i, 0, 0))
        u_shape = jax.ShapeDtypeStruct((m // tm, LRU_CONV_W - 1, D_RNN), F32)
        hl_shape = jax.ShapeDtypeStruct((m // tm, 1, D_RNN), F32)
        scratch += [pltpu.VMEM((SUBLANES, D_RNN), F32)] * 2
    return pl.pallas_call(
        functools.partial(_lru_kernel, tm=tm, tiles_per_seq=tiles_per_seq, paged_prev=paged_prev),
        grid=(m // tm,),
        in_specs=in_specs,
        out_specs=(rows(D_RNN), u_spec, hl_spec),
        out_shape=(jax.ShapeDtypeStruct((m, D_RNN), BF16), u_shape, hl_shape),
        scratch_shapes=scratch,
        compiler_params=_params(1),
        name="lru",
    )(*args)


def _rope_tables(pos):
    half = ROT_DIMS // 2
    inv = ROPE_THETA ** (-jnp.arange(half, dtype=F32) / half)
    ang = pos.astype(F32)[:, None] * inv[None, :]
    lane = jnp.arange(LANES) % DH_DIFF
    cos = jnp.cos(ang)[:, lane % half]
    sin = jnp.sin(ang)[:, lane % half]
    cos_t = jnp.where(lane[None, :] < ROT_DIMS, cos, 1.0)
    sin_t = jnp.where(lane[None, :] < half, -sin, jnp.where(lane[None, :] < ROT_DIMS, sin, 0.0))
    return cos_t, sin_t


def _block_diag(w):
    n, c, d = w.shape
    same = jnp.arange(n)[:, None] == jnp.arange(n)[None, :]
    return jnp.where(same[:, None, :, None], w[:, :, None, :], 0).reshape(n * c, n * d)


def _expand_prev(buf, t, k):
    bs, wm1, c = buf.shape
    n = min(k, t)
    return jnp.pad(buf[:, wm1 - k:wm1 - k + n], ((0, 0), (0, t - n), (0, 0))).reshape(bs * t, c)


def kernel(x_prompt, x_sample, cache_fox_k, cache_fox_v, cache_fox_logf, cache_diff_k, cache_diff_v, state_lru_conv, state_lru_h, state_ffn_conv, page_table, mix_norm_g, ab_w_in, ab_b_f, ab_lam_q1, ab_lam_k1, ab_lam_q2, ab_lam_k2, ab_subln_g, ab_w_out, lru_w_gate, lru_w_x, lru_conv_w, lru_conv_b, lru_w_a, lru_b_a, lru_w_i, lru_b_i, lru_lambda, lru_w_out, ffn_norm_g, ffn_w_a, ffn_w_b, ffn_conv_w, ffn_conv_b, ffn_w_down, final_norm_g):
    depth = mix_norm_g.shape[0]
    assert depth == 2 and ab_w_in.shape[0] == 1 and lru_w_x.shape[0] == 1

    w_in = ab_w_in[0]
    o_f = 3 * D_FOX
    w_all = jnp.concatenate([w_in[:, :o_f], w_in[:, o_f + H_FOX:], w_in[:, o_f:o_f + H_FOX],
                             jnp.zeros((D_MODEL, LANES - H_FOX), F32)], axis=1).astype(BF16)
    bf_pad = jnp.pad(ab_b_f[0], (0, LANES - H_FOX)).reshape(1, LANES)
    lamv = jnp.stack([ab_lam_q1[0], ab_lam_k1[0], ab_lam_q2[0], ab_lam_k2[0]]).astype(F32)
    subln_g = ab_subln_g[0].reshape(1, LANES)
    w_out = ab_w_out[0].astype(BF16)
    wg = lru_w_gate[0].astype(BF16)
    wx = lru_w_x[0].astype(BF16)
    wri = jnp.concatenate([_block_diag(lru_w_a[0]), _block_diag(lru_w_i[0])], axis=1).astype(BF16)
    bri = jnp.concatenate([lru_b_a[0], lru_b_i[0]]).reshape(1, 2 * D_RNN)
    lru_wo = lru_w_out[0].astype(BF16)
    ffn_wa = ffn_w_a.astype(BF16)
    ffn_wb = ffn_w_b.astype(BF16)
    ffn_wd = ffn_w_down.astype(BF16)
    lam_init = 0.8 - 0.6 * math.exp(-0.3 * 0)

    def layer0_proj(x2d, pos_rows, tm, seq_len=None):
        cos_t, sin_t = _rope_tables(pos_rows)
        return _inproj(x2d, mix_norm_g[0].reshape(1, D_MODEL), w_all, bf_pad, cos_t, sin_t, tm, seq_len)

    def tail(x2d, mix0, b, t, tm_ffn, tm_lru, tf, ffn_prev, lru_prev):
        x1, nb0 = _ffn(x2d, mix0, w_out, ffn_norm_g[0].reshape(1, D_MODEL), ffn_wa[0], ffn_wb[0],
                       ffn_conv_w[0], ffn_conv_b[0].reshape(1, D_FF), ffn_wd[0], b, t, tm_ffn, tf,
                       prev=ffn_prev[0])
        mix1, u_out, h_out = _lru(x1, mix_norm_g[1].reshape(1, D_MODEL), wg, wx, lru_conv_w[0],
                                  lru_conv_b[0].reshape(1, D_RNN), wri, bri, lru_lambda[0].reshape(1, D_RNN),
                                  b, t, tm_lru, prev=lru_prev)
        y, nb1 = _ffn(x1, (mix1,), lru_wo, ffn_norm_g[1].reshape(1, D_MODEL), ffn_wa[1], ffn_wb[1],
                      ffn_conv_w[1], ffn_conv_b[1].reshape(1, D_FF), ffn_wd[1], b, t, tm_ffn, tf,
                      prev=ffn_prev[1], final_g=final_norm_g.reshape(1, D_MODEL))
        return y, nb0, nb1, u_out, h_out

    bp, tp, _ = x_prompt.shape
    mp = bp * tp
    xp = x_prompt.reshape(mp, D_MODEL)
    (qf, kf, vf, kfb, vfb, logf, logft, qd, kd, vd, kdb, vdb) = layer0_proj(xp, jnp.arange(tp), 512, tp)
    fox_state = lambda a: jnp.transpose(a.reshape(bp, H_FOX, DH_FOX, tp), (0, 3, 1, 2))[None]
    cum = _cumsum_prompt(logft, bp, tp)
    r3 = lambda a: a.reshape(bp, tp, D_FOX)
    tq = 512
    of = _attn_prompt(r3(qf), r3(kfb), r3(vfb), (cum,), True, tq, lam_init)
    od = _attn_prompt(r3(qd), r3(kdb), r3(vdb), (lamv, subln_g), False, tq, lam_init)
    mix0 = (of.reshape(mp, D_FOX), od.reshape(mp, D_DIFF))
    tm_ffn, tm_lru = 256, 256
    y_p, nb0, nb1, u_out, h_out = tail(xp, mix0, bp, tp, tm_ffn, tm_lru, 512, (None, None), None)
    seq_last = lambda a, tm: a[tp // tm - 1::tp // tm]
    nb0, nb1, u_out, h_out = seq_last(nb0, tm_ffn), seq_last(nb1, tm_ffn), seq_last(u_out, tm_lru), seq_last(h_out, tm_lru)
    y_prompt = y_p.reshape(bp, tp, D_MODEL)
    p_state = (fox_state(kf), fox_state(vf),
               logf.reshape(1, bp, tp, H_FOX),
               kd.reshape(1, bp, tp, H_DIFF, 2 * DH_DIFF), vd.reshape(1, bp, tp, H_DIFF, 2 * DH_DIFF),
               u_out[None], h_out.reshape(1, bp, D_RNN), jnp.stack([nb0, nb1]))

    bs, ts, _ = x_sample.shape
    assert ts == SUBLANES
    ms = bs * ts
    n_pages = page_table.shape[1]
    past = n_pages * PAGE_SIZE
    xs = x_sample.reshape(ms, D_MODEL)
    pos_rows = jnp.tile(past + jnp.arange(ts), bs)
    (qf, kf, vf, kfb, vfb, logf, logft, qd, kd, vd, kdb, vdb) = layer0_proj(xs, pos_rows, ms)
    n_pool = cache_fox_k.shape[1]
    logf_pool = jnp.swapaxes(cache_fox_logf[0], 1, 2).reshape(n_pool * H_FOX, PAGE_SIZE)
    lc = _cumsum_rows(logf_pool).reshape(n_pool, H_FOX, PAGE_SIZE)
    logfn = jnp.pad(logft.reshape(H_FOX, bs, ts).transpose(1, 0, 2), ((0, 0), (0, 0), (0, LANES - ts)))
    pad_rows = lambda a: jnp.pad(a.reshape(bs, ts, D_FOX), ((0, 0), (0, PAGE_SIZE - ts), (0, 0)))
    kt_pages = lambda c: jnp.transpose(c, (0, 2, 3, 1)).reshape(n_pool, D_FOX, PAGE_SIZE)
    mix0 = _attn_decode(page_table, qf.astype(F32).reshape(bs, ts, D_FOX), qd.astype(F32).reshape(bs, ts, D_DIFF),
                        pad_rows(kfb), pad_rows(vfb), pad_rows(kdb), pad_rows(vdb),
                        kt_pages(cache_fox_k[0]), kt_pages(cache_fox_v[0]), lc,
                        cache_diff_k[0].reshape(n_pool, PAGE_SIZE * H_DIFF, 2 * DH_DIFF),
                        cache_diff_v[0].reshape(n_pool, PAGE_SIZE * H_DIFF, 2 * DH_DIFF),
                        logfn, lamv, subln_g, 16, lam_init).reshape(ms, D_MIX_AB)
    ffn_prev = tuple((_expand_prev(state_ffn_conv[l], ts, 1), _expand_prev(state_ffn_conv[l], ts, 2))
                     for l in range(depth))
    lru_prev = tuple(_expand_prev(state_lru_conv[0], ts, k) for k in (1, 2, 3)) + (state_lru_h[0],)
    y_s, a0, a1, u_full, h_full = tail(xs, (mix0,), bs, ts, ms, ms, 512, ffn_prev, lru_prev)
    y_sample = y_s.reshape(bs, ts, D_MODEL)
    last = lambda a, k: a.reshape(bs, ts, -1)[:, ts - k:]
    s_state = (kf.reshape(1, bs, ts, H_FOX, DH_FOX), vf.reshape(1, bs, ts, H_FOX, DH_FOX),
               logf.reshape(1, bs, ts, H_FOX),
               kd.reshape(1, bs, ts, H_DIFF, 2 * DH_DIFF), vd.reshape(1, bs, ts, H_DIFF, 2 * DH_DIFF),
               last(u_full, LRU_CONV_W - 1)[None], last(h_full, 1).reshape(1, bs, D_RNN),
               jnp.stack([last(a0, FFN_CONV_W - 1), last(a1, FFN_CONV_W - 1)]))

    return (y_prompt, y_sample) + p_state + s_state
```

```python
import functools
import math

import jax
import jax.numpy as jnp
from jax import lax
from jax.experimental import pallas as pl
from jax.experimental.pallas import tpu as pltpu

F32 = jnp.float32
BF16 = jnp.bfloat16

D_MODEL = 1024
H_FOX = 8
DH_FOX = 64
H_DIFF = 4
DH_DIFF = 64
D_FOX = H_FOX * DH_FOX
D_DIFF = H_DIFF * 2 * DH_DIFF
D_MIX_AB = D_FOX + D_DIFF
ROT_DIMS = DH_DIFF // 4
ROPE_THETA = 500000.0
D_RNN = 1280
N_LRU_BLOCKS = 16
LRU_BW = D_RNN // N_LRU_BLOCKS
LRU_CONV_W = 4
LRU_C = 8.0
D_FF = 3072
FFN_CONV_W = 3
EPS = 1e-6
NEG = -1e30
PAGE_SIZE = 128
LOG2E = math.log2(math.e)

LANES = 128
SUBLANES = 8
VMEM_LIMIT = 56 * 1024 * 1024
N_QKV = 6 * D_FOX
N_INPROJ = N_QKV + LANES


def _params(n_axes):
    return pltpu.CompilerParams(dimension_semantics=("arbitrary",) * n_axes,
                                vmem_limit_bytes=VMEM_LIMIT)


def _full(shape):
    n = len(shape)
    return pl.BlockSpec(shape, lambda *_: (0,) * n)


def _resident(shape):
    n = len(shape)
    return pl.BlockSpec(shape, lambda *_: (0,) * n, pipeline_mode=pl.Buffered(1))


def _rms(x, g):
    return x * lax.rsqrt(jnp.mean(x * x, axis=-1, keepdims=True) + EPS) * g


def _gelu(x):
    c = math.sqrt(2.0 / math.pi)
    return x * (0.5 * (1.0 + jnp.tanh(c * (x + 0.044715 * (x * x * x)))))


def _log_sigmoid(z):
    return -(jnp.maximum(-z, 0.0) + jnp.log1p(jnp.exp(-jnp.abs(z))))


def _sigmoid(z):
    return 0.5 * jnp.tanh(0.5 * z) + 0.5


def _dot(a, b):
    return jnp.dot(a, b, preferred_element_type=F32)


def _dot_nt(a, b):
    return lax.dot_general(a, b, (((1,), (1,)), ((), ())), preferred_element_type=F32)


def _prefix_lanes(x):
    r = lax.broadcasted_iota(jnp.int32, (LANES, LANES), 0)
    c = lax.broadcasted_iota(jnp.int32, (LANES, LANES), 1)
    tri = jnp.where(r <= c, 1.0, 0.0).astype(BF16)
    hi = x.astype(BF16)
    r1 = x - hi.astype(F32)
    mid = r1.astype(BF16)
    lo = (r1 - mid.astype(F32)).astype(BF16)
    return _dot(hi, tri) + _dot(mid, tri) + _dot(lo, tri)


def _inproj_kernel(x_ref, g_ref, w_ref, bf_ref, cos_ref, sin_ref,
                   qf_ref, kf_ref, vf_ref, kfb_ref, vfb_ref, logf_ref, logft_ref,
                   qd_ref, kd_ref, vd_ref, kdb_ref, vdb_ref):
    h = _rms(x_ref[...], g_ref[...]).astype(BF16)

    def proj(i, width=D_FOX):
        return _dot(h, w_ref[:, i * D_FOX:i * D_FOX + width])

    qf_ref[...] = (proj(0) * (DH_FOX ** -0.5 * LOG2E)).astype(BF16)
    def store_state(o_ref, x):
        if len(o_ref.shape) == 3:
            o_ref[0] = x.T
        else:
            o_ref[...] = x

    kf = proj(1)
    store_state(kf_ref, kf)
    kfb_ref[...] = kf.astype(BF16)
    vf = proj(2)
    store_state(vf_ref, vf)
    vfb_ref[...] = vf.astype(BF16)

    logf = _log_sigmoid(proj(6, LANES) + bf_ref[...])
    logf_ref[...] = logf[:, :H_FOX]
    logft_ref[...] = logf.T[:H_FOX, :]

    cosf = jnp.concatenate([cos_ref[...]] * (D_DIFF // LANES), axis=1)
    sinf = jnp.concatenate([sin_ref[...]] * (D_DIFF // LANES), axis=1)
    lane = lax.broadcasted_iota(jnp.int32, cosf.shape, 1) & (DH_DIFF - 1)
    half = ROT_DIMS // 2

    def rope(x):
        partner = jnp.where(lane < half, pltpu.roll(x, D_DIFF - half, 1), pltpu.roll(x, half, 1))
        return jnp.where(lane < ROT_DIMS, x * cosf + partner * sinf, x)

    def store_heads(o_ref, x):
        for hd in range(H_DIFF):
            o_ref[pl.ds(hd, x.shape[0], stride=H_DIFF), :] = x[:, hd * LANES:(hd + 1) * LANES]

    qd_ref[...] = (rope(proj(3)) * (DH_DIFF ** -0.5 * LOG2E)).astype(BF16)
    kd = rope(proj(4))
    store_heads(kd_ref, kd)
    kdb_ref[...] = kd.astype(BF16)
    vd = proj(5)
    store_heads(vd_ref, vd)
    vdb_ref[...] = vd.astype(BF16)


def _inproj(x2d, g, w_all, bf_pad, cos_t, sin_t, tm, seq_len=None):
    m = x2d.shape[0]
    nt = cos_t.shape[0] // tm
    row = lambda n: pl.BlockSpec((tm, n), lambda i: (i, 0))
    tab = pl.BlockSpec((tm, LANES), lambda i: (i % nt, 0))
    b16o = jax.ShapeDtypeStruct((m, D_FOX), BF16)
    if seq_len is None:
        fox_o, fox_spec = jax.ShapeDtypeStruct((m, D_FOX), F32), row(D_FOX)
    else:
        npt = seq_len // tm
        fox_o = jax.ShapeDtypeStruct((m // seq_len, D_FOX, seq_len), F32)
        fox_spec = pl.BlockSpec((1, D_FOX, tm), lambda i: (i // npt, 0, i % npt))
    heads_o = jax.ShapeDtypeStruct((m * H_DIFF, LANES), F32)
    heads_spec = pl.BlockSpec((tm * H_DIFF, LANES), lambda i: (i, 0))
    out_shape = (b16o, fox_o, fox_o, b16o, b16o,
                 jax.ShapeDtypeStruct((m, H_FOX), F32), jax.ShapeDtypeStruct((H_FOX, m), F32),
                 b16o, heads_o, heads_o, b16o, b16o)
    out_specs = ((row(D_FOX), fox_spec, fox_spec, row(D_FOX), row(D_FOX))
                 + (row(H_FOX), pl.BlockSpec((H_FOX, tm), lambda i: (0, i)))
                 + (row(D_FOX), heads_spec, heads_spec, row(D_FOX), row(D_FOX)))
    return pl.pallas_call(
        _inproj_kernel,
        grid=(m // tm,),
        in_specs=[row(D_MODEL), _full((1, D_MODEL)), _full((D_MODEL, N_INPROJ)), _full((1, LANES)), tab, tab],
        out_specs=out_specs,
        out_shape=out_shape,
        compiler_params=_params(1),
        name="inproj",
    )(x2d, g, w_all, bf_pad, cos_t, sin_t)


def _cumsum_prompt_kernel(x_ref, o_ref, *, t):
    carry = jnp.zeros((H_FOX, 1), F32)
    for i in range(t // LANES):
        c = _prefix_lanes(x_ref[:, i * LANES:(i + 1) * LANES]) + carry
        c2 = c * LOG2E
        for j in range(H_FOX // 2):
            o_ref[0, j, :, i * LANES:(i + 1) * LANES] = c2[2 * j:2 * j + 2, :]
        carry = c[:, LANES - 1:LANES]


def _cumsum_prompt(logft, b, t):
    return pl.pallas_call(
        functools.partial(_cumsum_prompt_kernel, t=t),
        grid=(b,),
        in_specs=[pl.BlockSpec((H_FOX, t), lambda i: (0, i))],
        out_specs=pl.BlockSpec((1, H_FOX // 2, 2, t), lambda i: (i, 0, 0, 0)),
        out_shape=jax.ShapeDtypeStruct((b, H_FOX // 2, 2, t), F32),
        compiler_params=_params(1),
        name="cumsum_prompt",
    )(logft)


def _cumsum_rows_kernel(x_ref, o_ref):
    o_ref[...] = _prefix_lanes(x_ref[...])


def _cumsum_rows(x, max_rows=2048):
    rows = x.shape[0]
    tr = max(d for d in range(SUBLANES, max_rows + 1, SUBLANES) if rows % d == 0)
    spec = pl.BlockSpec((tr, LANES), lambda i: (i, 0))
    return pl.pallas_call(
        _cumsum_rows_kernel,
        grid=(rows // tr,),
        in_specs=[spec],
        out_specs=spec,
        out_shape=jax.ShapeDtypeStruct((rows, LANES), F32),
        compiler_params=_params(1),
        name="cumsum_pages",
    )(x)


def _stack_masked(q):
    lane = lax.broadcasted_iota(jnp.int32, q.shape, 1)
    zero = jnp.zeros_like(q)
    return jnp.concatenate([jnp.where(lane < LANES // 2, q, zero),
                            jnp.where(lane >= LANES // 2, q, zero)], axis=0)


def _lambda(lamv, lam_init):
    s1 = jnp.sum(lamv[0:1] * lamv[1:2], axis=1, keepdims=True)
    s2 = jnp.sum(lamv[2:3] * lamv[3:4], axis=1, keepdims=True)
    return jnp.exp(s1) - jnp.exp(s2) + lam_init


def _fox_out(acc, l, r):
    o = acc / l
    lane = lax.broadcasted_iota(jnp.int32, (r, LANES), 1)
    return jnp.where(lane < LANES // 2, o[:r], o[r:])


def _diff_out(acc, l, r, lamv, g, lam_init):
    o = acc / l
    od = o[:r] - _lambda(lamv, lam_init) * o[r:]
    return _rms(od, g) * (1.0 - lam_init)


def _attn_prompt_kernel(*refs, is_fox, tq, rc, lam_init):
    if is_fox:
        q_ref, k_ref, v_ref, c_ref, o_ref = refs[:5]
    else:
        q_ref, k_ref, v_ref, lamv_ref, g_ref, o_ref = refs[:6]
    s0_s, s1_s, r0_s, r1_s, m_s, l_s, acc_s = refs[-7:]
    qi = pl.program_id(2)
    q2 = _stack_masked(q_ref[0])
    nrep = tq // LANES

    def keys(kb):
        return pl.ds(pl.multiple_of(kb * tq, tq), tq)

    def row_max(s):
        return jnp.broadcast_to(jnp.max(s, axis=1, keepdims=True), (2 * tq, LANES))

    def qk(kb, s_s, r_s):
        s = _dot_nt(q2, k_ref[0, keys(kb), :])
        if is_fox:
            c0 = c_ref[0, 0, 0:1, keys(kb)]
            c1 = c_ref[0, 0, 1:2, keys(kb)]
            s = jnp.concatenate([s[:tq] - c0, s[tq:] - c1], axis=0)
        s_s[...] = s
        r_s[...] = row_max(s)

    def softmax_pv(kb, s_s, r_s):
        m_old = m_s[...]
        m_new = jnp.maximum(m_old, r_s[...])
        alpha = jnp.exp2(m_old - m_new)
        m_s[...] = m_new
        sums, probs = [], []
        for ci in range(2 * tq // rc):
            rows = slice(ci * rc, (ci + 1) * rc)
            p = jnp.exp2((s_s[rows, :] - jnp.tile(m_new[rows], (1, nrep))).astype(BF16))
            sums.append(sum(p[:, i * LANES:(i + 1) * LANES] for i in range(nrep)).astype(F32))
            probs.append(p)
        l_s[...] = alpha * l_s[...] + jnp.concatenate(sums, axis=0)
        acc_s[...] = alpha * acc_s[...] + _dot(jnp.concatenate(probs, axis=0), v_ref[0, keys(kb), :])

    m_s[...] = jnp.full(m_s.shape, NEG, F32)
    l_s[...] = jnp.zeros(l_s.shape, F32)
    acc_s[...] = jnp.zeros(acc_s.shape, F32)
    bufs = ((s0_s, r0_s), (s1_s, r1_s))
    qk(0, *bufs[0])

    def body(kb, _):
        for parity in range(2):
            @pl.when((kb & 1) == parity)
            def _():
                qk(kb + 1, *bufs[1 - parity])
                softmax_pv(kb, *bufs[parity])
        return 0

    lax.fori_loop(0, qi, body, 0)
    row = lax.broadcasted_iota(jnp.int32, (2 * tq, tq), 0)
    col = lax.broadcasted_iota(jnp.int32, (2 * tq, tq), 1)
    causal = col <= jnp.where(row >= tq, row - tq, row)
    for parity in range(2):
        @pl.when((qi & 1) == parity)
        def _():
            s_s, r_s = bufs[parity]
            s = jnp.where(causal, s_s[...], NEG)
            s_s[...] = s
            r_s[...] = row_max(s)
            softmax_pv(qi, s_s, r_s)

    acc = acc_s[...]
    l = jnp.sum(l_s[...], axis=1, keepdims=True)
    if is_fox:
        o_ref[0] = _fox_out(acc, l, tq).astype(BF16)
    else:
        o_ref[0] = _diff_out(acc, l, tq, lamv_ref[...], g_ref[...], lam_init).astype(BF16)


def _attn_prompt(q, k, v, extra, is_fox, tq, lam_init, rc=32):
    b, t, _ = q.shape
    nblk = D_FOX // LANES
    wide = pltpu.VMEM((2 * tq, LANES), F32)
    scratch = [pltpu.VMEM((2 * tq, tq), F32), pltpu.VMEM((2 * tq, tq), F32), wide, wide, wide, wide, wide]
    qspec = pl.BlockSpec((1, tq, LANES), lambda bi, j, qi: (bi, qi, j))
    kvspec = pl.BlockSpec((1, t, LANES), lambda bi, j, qi: (bi, 0, j))
    if is_fox:
        especs = [pl.BlockSpec((1, 1, 2, t), lambda bi, j, qi: (bi, j, 0, 0))]
    else:
        especs = [_full((4, DH_DIFF)), _full((1, LANES))]
    return pl.pallas_call(
        functools.partial(_attn_prompt_kernel, is_fox=is_fox, tq=tq, rc=rc, lam_init=lam_init),
        grid=(b, nblk, t // tq),
        in_specs=[qspec, kvspec, kvspec] + especs,
        out_specs=qspec,
        out_shape=jax.ShapeDtypeStruct((b, t, D_FOX), BF16),
        scratch_shapes=scratch,
        compiler_params=_params(3),
        name="attn_prompt_fox" if is_fox else "attn_prompt_diff",
    )(q, k, v, *extra)


def _attn_decode_kernel(pt_ref, *refs, g_pages, n_steps, t_new, lam_init):
    n_in = 6 + 5 * g_pages + 5
    (qf_ref, qd_ref, kfn_ref, vfn_ref, kdn_ref, vdn_ref) = refs[:6]
    pages = refs[6:6 + 5 * g_pages]
    fk_refs, fv_refs, lc_refs, dk_refs, dv_refs = (pages[i * g_pages:(i + 1) * g_pages] for i in range(5))
    logfn_ref, lamv_ref, g_ref = refs[6 + 5 * g_pages:6 + 5 * g_pages + 3]
    o_ref = refs[n_in - 2]
    q2_s, m_s, l_s, acc_s, cc_s = refs[n_in - 1:]
    del pt_ref
    p = pl.program_id(1)
    nfb = D_FOX // LANES
    ndb = D_DIFF // LANES
    r = SUBLANES

    nblk = nfb + ndb
    rb = 2 * r

    def blk(x, j):
        return x[j * rb:(j + 1) * rb]

    def lanes(x, j):
        return x[:, j * LANES:(j + 1) * LANES]

    @pl.when(p == 0)
    def _():
        q = jnp.concatenate([qf_ref[0], qd_ref[0]], axis=1)
        q2_s[...] = jnp.concatenate([_stack_masked(lanes(q, j)) for j in range(nblk)], axis=0).astype(BF16)
        m_s[...] = jnp.full(m_s.shape, NEG, F32)
        l_s[...] = jnp.zeros(l_s.shape, F32)
        acc_s[...] = jnp.zeros(acc_s.shape, F32)
        cc_s[...] = jnp.zeros(cc_s.shape, F32)

    def fox_bias(ck):
        ck = ck * LOG2E
        return jnp.concatenate([jnp.broadcast_to(ck[h:h + 1], (r, ck.shape[1])) for h in range(H_FOX)], axis=0)

    nf = nfb * rb
    nd = ndb * rb

    def update(lo, n, u, pv_fn):
        sl = slice(lo, lo + n)
        m = m_s[sl]
        m_new = jnp.maximum(m, jnp.max(u, axis=1, keepdims=True))
        alpha = jnp.exp2(m - m_new)
        pe = jnp.exp2(u - m_new)
        m_s[sl] = m_new
        l_s[sl] = alpha * l_s[sl] + jnp.sum(pe, axis=1, keepdims=True)
        acc_s[sl] = alpha * acc_s[sl] + pv_fn(pe.astype(BF16))

    @pl.when(p < n_steps)
    def _():
        cks = []
        run = cc_s[...]
        for gi in range(g_pages):
            lc = lc_refs[gi][0]
            cks.append(run + lc)
            run = run + lc[:, LANES - 1:LANES]
        cc_s[...] = run
        q2 = q2_s[...]

        def fkv(refs, gi, j):
            return refs[gi][0, j * LANES:(j + 1) * LANES, :].astype(BF16)

        s_fox = jnp.concatenate(
            [jnp.concatenate([_dot(blk(q2, j), fkv(fk_refs, gi, j)) for gi in range(g_pages)], axis=1)
             for j in range(nfb)], axis=0)

        def pv_fox(pb):
            return jnp.concatenate(
                [sum(_dot_nt(lanes(blk(pb, j), gi), fkv(fv_refs, gi, j)) for gi in range(g_pages))
                 for j in range(nfb)], axis=0)

        update(0, nf, s_fox - fox_bias(jnp.concatenate(cks, axis=1)), pv_fox)

        wd = PAGE_SIZE * H_DIFF
        s_diff = jnp.concatenate([_dot_nt(q2[nf:], dk_refs[gi][0].astype(BF16)) for gi in range(g_pages)], axis=1)
        row = lax.broadcasted_iota(jnp.int32, s_diff.shape, 0)
        col = lax.broadcasted_iota(jnp.int32, s_diff.shape, 1)
        own = (col & (H_DIFF - 1)) == row // rb

        def pv_diff(pb):
            return sum(_dot(pb[:, gi * wd:(gi + 1) * wd], dv_refs[gi][0].astype(BF16)) for gi in range(g_pages))

        update(nf, nd, jnp.where(own, s_diff, NEG), pv_diff)

    @pl.when(p == n_steps)
    def _():
        row = lax.broadcasted_iota(jnp.int32, (nf, LANES), 0) & (r - 1)
        col = lax.broadcasted_iota(jnp.int32, (nf, LANES), 1)
        keep = (col <= row) & (col < t_new)
        ck = cc_s[...] + _prefix_lanes(logfn_ref[0])
        q2 = q2_s[...]
        s_fox = jnp.concatenate([_dot_nt(blk(q2, j), lanes(kfn_ref[0], j)) for j in range(nfb)], axis=0)
        update(0, nf, jnp.where(keep, s_fox - fox_bias(ck), NEG),
               lambda pb: jnp.concatenate([_dot(blk(pb, j), lanes(vfn_ref[0], j)) for j in range(nfb)], axis=0))
        s_diff = jnp.concatenate([_dot_nt(blk(q2, nfb + j), lanes(kdn_ref[0], j)) for j in range(ndb)], axis=0)
        update(nf, nd, jnp.where(keep, s_diff, NEG),
               lambda pb: jnp.concatenate([_dot(blk(pb, j), lanes(vdn_ref[0], j)) for j in range(ndb)], axis=0))
        acc = acc_s[...]
        l = l_s[...]
        lamv = lamv_ref[...]
        for j in range(nfb):
            o_ref[0, :, j * LANES:(j + 1) * LANES] = _fox_out(blk(acc, j), blk(l, j), r).astype(BF16)
        for j in range(nfb, nblk):
            o = _diff_out(blk(acc, j), blk(l, j), r, lamv, g_ref[...], lam_init)
            o_ref[0, :, j * LANES:(j + 1) * LANES] = o.astype(BF16)


def _attn_decode(page_table, qf, qd, kfn, vfn, kdn, vdn, fk, fv, lc, dk, dv, logfn, lamv, g, g_pages, lam_init):
    bs, t_new, _ = qf.shape
    n_pages = page_table.shape[1]
    n_steps = n_pages // g_pages
    nblk = (D_FOX + D_DIFF) // LANES

    def seq(shape):
        return pl.BlockSpec((1,) + shape, lambda b, p, pt: (b, 0, 0))

    def paged(shape, gi):
        def index(b, p, pt):
            return (pt[b, jnp.minimum(p, n_steps - 1) * g_pages + gi],) + (0,) * len(shape)
        return pl.BlockSpec((1,) + shape, index)

    in_specs = [seq((t_new, D_FOX)), seq((t_new, D_DIFF))] + [seq((PAGE_SIZE, D_FOX))] * 4
    args = [qf, qd, kfn, vfn, kdn, vdn]
    for arr in (fk, fv, lc, dk, dv):
        for gi in range(g_pages):
            in_specs.append(paged(arr.shape[1:], gi))
            args.append(arr)
    in_specs += [seq((H_FOX, LANES)),
                 pl.BlockSpec((4, DH_DIFF), lambda b, p, pt: (0, 0)),
                 pl.BlockSpec((1, LANES), lambda b, p, pt: (0, 0))]
    args += [logfn, lamv, g]
    grid_spec = pltpu.PrefetchScalarGridSpec(
        num_scalar_prefetch=1,
        grid=(bs, n_steps + 1),
        in_specs=in_specs,
        out_specs=seq((t_new, D_MIX_AB)),
        scratch_shapes=[pltpu.VMEM((nblk * 2 * t_new, LANES), BF16),
                        pltpu.VMEM((nblk * 2 * t_new, 1), F32),
                        pltpu.VMEM((nblk * 2 * t_new, 1), F32),
                        pltpu.VMEM((nblk * 2 * t_new, LANES), F32),
                        pltpu.VMEM((H_FOX, LANES), F32)],
    )
    return pl.pallas_call(
        functools.partial(_attn_decode_kernel, g_pages=g_pages, n_steps=n_steps, t_new=t_new, lam_init=lam_init),
        grid_spec=grid_spec,
        out_shape=jax.ShapeDtypeStruct((bs, t_new, D_MIX_AB), BF16),
        compiler_params=_params(2),
        name="attn_decode",
    )(page_table, *args)


def _ffn_kernel(*refs, tm, tf, n_mix, tiles_per_seq, paged_prev, final_norm):
    it = iter(refs)
    x_ref = next(it)
    mix_refs = [next(it) for _ in range(n_mix)]
    wo_ref, g_ref, wa_ref, wb_ref, cw_ref, cb_ref, wd_ref = (next(it) for _ in range(7))
    if paged_prev:
        p1_ref, p2_ref = next(it), next(it)
    if final_norm:
        fg_ref = next(it)
    o_ref, nb_ref = next(it), next(it)
    if not paged_prev:
        carry_s = next(it)

        @pl.when((pl.program_id(0) % tiles_per_seq) == 0)
        def _():
            carry_s[...] = jnp.zeros(carry_s.shape, F32)

    y = None
    k0 = 0
    for mix_ref in mix_refs:
        k1 = k0 + mix_ref.shape[1]
        part = _dot(mix_ref[...], wo_ref[k0:k1, :])
        y = part if y is None else part + y
        k0 = k1
    x1 = x_ref[...] + y
    h = _rms(x1, g_ref[...]).astype(BF16)
    nf = D_FF // tf
    row = lax.broadcasted_iota(jnp.int32, (tm if paged_prev else SUBLANES, tf), 0)

    def up(f):
        cols = slice(f * tf, (f + 1) * tf)
        return _dot(h, wa_ref[:, cols]), _dot(h, wb_ref[:, cols])

    def act(f, a, gate):
        cols = slice(f * tf, (f + 1) * tf)
        a1 = pltpu.roll(a, 1, 0)
        a2 = pltpu.roll(a, 2, 0)
        if paged_prev:
            t = row & (SUBLANES - 1)
            a1 = jnp.where(t == 0, p1_ref[:, cols], a1)
            a2 = jnp.where(t < 2, p2_ref[:, cols], a2)
            nb_ref[:, cols] = a
        else:
            c = carry_s[:, cols]
            top1 = jnp.where(row == 0, c[SUBLANES - 1:], a1[:SUBLANES])
            top2 = jnp.where(row == 0, c[SUBLANES - 2:SUBLANES - 1],
                             jnp.where(row == 1, c[SUBLANES - 1:], a2[:SUBLANES]))
            a1 = jnp.concatenate([top1, a1[SUBLANES:]], axis=0)
            a2 = jnp.concatenate([top2, a2[SUBLANES:]], axis=0)
            carry_s[:, cols] = a[tm - SUBLANES:, :]
            nb_ref[0, :, cols] = a[tm - (FFN_CONV_W - 1):, :]
        cw = cw_ref[:, cols]
        ac = cb_ref[:, cols] + a2 * cw[0:1] + a1 * cw[1:2] + a * cw[2:3]
        return (_gelu(ac) * gate).astype(BF16)

    acc = x1
    pending = up(0)
    for f in range(nf):
        following = up(f + 1) if f + 1 < nf else None
        acc = _dot(act(f, *pending), wd_ref[f * tf:(f + 1) * tf, :]) + acc
        pending = following
    if final_norm:
        acc = _rms(acc, fg_ref[...])
    o_ref[...] = acc


def _ffn(x2d, mix_parts, wo, g, wa, wb, cw, cb, wd, b, t, tm, tf, prev=None, final_g=None):
    m = x2d.shape[0]
    paged_prev = prev is not None
    tiles_per_seq = max(t // tm, 1)
    rows = lambda n: pl.BlockSpec((tm, n), lambda i: (i, 0))
    in_specs = ([rows(D_MODEL)] + [rows(p.shape[1]) for p in mix_parts]
                + [_resident(wo.shape), _resident((1, D_MODEL)),
                   _resident((D_MODEL, D_FF)), _resident((D_MODEL, D_FF)), _resident((FFN_CONV_W, D_FF)),
                   _resident((1, D_FF)), _resident((D_FF, D_MODEL))])
    args = [x2d, *mix_parts, wo, g, wa, wb, cw, cb, wd]
    scratch = []
    if paged_prev:
        in_specs += [rows(D_FF)] * 2
        args += list(prev)
        nb_spec = rows(D_FF)
        nb_shape = jax.ShapeDtypeStruct((m, D_FF), F32)
    else:
        nb_spec = pl.BlockSpec((1, FFN_CONV_W - 1, D_FF), lambda i: (i, 0, 0))
        nb_shape = jax.ShapeDtypeStruct((m // tm, FFN_CONV_W - 1, D_FF), F32)
        scratch.append(pltpu.VMEM((SUBLANES, D_FF), F32))
    if final_g is not None:
        in_specs.append(_resident((1, D_MODEL)))
        args.append(final_g)
    return pl.pallas_call(
        functools.partial(_ffn_kernel, tm=tm, tf=tf, n_mix=len(mix_parts), tiles_per_seq=tiles_per_seq,
                          paged_prev=paged_prev, final_norm=final_g is not None),
        grid=(m // tm,),
        in_specs=in_specs,
        out_specs=(rows(D_MODEL), nb_spec),
        out_shape=(jax.ShapeDtypeStruct((m, D_MODEL), F32), nb_shape),
        scratch_shapes=scratch,
        compiler_params=_params(1),
        name="ffn",
    )(*args)


def _lru_kernel(*refs, tm, tiles_per_seq, paged_prev):
    it = iter(refs)
    x_ref, g_ref, wg_ref, wx_ref, cw_ref, cb_ref, wri_ref, bri_ref, lam_ref = (next(it) for _ in range(9))
    if paged_prev:
        p1_ref, p2_ref, p3_ref, h0_ref = (next(it) for _ in range(4))
    mix_ref, u_ref, hl_ref = next(it), next(it), next(it)
    if not paged_prev:
        cu_s, ch_s = next(it), next(it)
    mi = pl.program_id(0)

    h = _rms(x_ref[...], g_ref[...]).astype(BF16)
    u = _dot(h, wx_ref[...])
    gate_in = _dot(h, wg_ref[...])
    u1, u2, u3 = (pltpu.roll(u, k, 0) for k in (1, 2, 3))
    if paged_prev:
        t = lax.broadcasted_iota(jnp.int32, (tm, D_RNN), 0) & (SUBLANES - 1)
        u1 = jnp.where(t == 0, p1_ref[...], u1)
        u2 = jnp.where(t < 2, p2_ref[...], u2)
        u3 = jnp.where(t < 3, p3_ref[...], u3)
        u_ref[...] = u
    else:
        @pl.when((mi % tiles_per_seq) == 0)
        def _():
            cu_s[...] = jnp.zeros(cu_s.shape, F32)
            ch_s[...] = jnp.zeros(ch_s.shape, F32)

        c = cu_s[...]
        row = lax.broadcasted_iota(jnp.int32, (SUBLANES, D_RNN), 0)
        l1, l2, l3 = (c[SUBLANES - k:SUBLANES - k + 1] for k in (1, 2, 3))
        top1 = jnp.where(row == 0, l1, u1[:SUBLANES])
        top2 = jnp.where(row == 0, l2, jnp.where(row == 1, l1, u2[:SUBLANES]))
        top3 = jnp.where(row == 0, l3, jnp.where(row == 1, l2, jnp.where(row == 2, l1, u3[:SUBLANES])))
        u1, u2, u3 = (jnp.concatenate([top, full[SUBLANES:]], axis=0)
                      for top, full in ((top1, u1), (top2, u2), (top3, u3)))
        cu_s[...] = u[tm - SUBLANES:, :]
        u_ref[0] = u[tm - (LRU_CONV_W - 1):, :]
    cw = cw_ref[...]
    xc = cb_ref[...] + u3 * cw[0:1] + u2 * cw[1:2] + u1 * cw[2:3] + u * cw[3:4]

    xb = xc.astype(BF16)
    tile = 2 * LANES
    nt = D_RNN // tile
    near = lambda ct: [kc for kc in (ct - 1, ct, ct + 1) if 0 <= kc < nt]

    def gate_proj(off):
        tiles = []
        for ct in range(nt):
            cols = slice(off + ct * tile, off + (ct + 1) * tile)
            acc = None
            for kc in near(ct):
                part = _dot(xb[:, kc * tile:(kc + 1) * tile], wri_ref[kc * tile:(kc + 1) * tile, cols])
                acc = part if acc is None else part + acc
            tiles.append(bri_ref[:, cols] + acc)
        return jnp.concatenate(tiles, axis=1)

    rg = _sigmoid(gate_proj(0))
    ig = _sigmoid(gate_proj(D_RNN))
    log_a = rg * (LRU_C * _log_sigmoid(lam_ref[...]))
    a_all = jnp.exp(log_a)
    bx_all = jnp.sqrt(-jnp.tanh(log_a) * (a_all * a_all + 1.0)) * (ig * xc)

    r8 = lax.broadcasted_iota(jnp.int32, (SUBLANES, D_RNN), 0)
    hc = None if paged_prev else ch_s[SUBLANES - 1:SUBLANES, :]
    groups = []
    for gi in range(tm // SUBLANES):
        av = a_all[gi * SUBLANES:(gi + 1) * SUBLANES]
        bv = bx_all[gi * SUBLANES:(gi + 1) * SUBLANES]
        for s in (1, 2, 4):
            keep = r8 >= s
            bv = jnp.where(keep, av * pltpu.roll(bv, s, 0) + bv, bv)
            av = jnp.where(keep, av * pltpu.roll(av, s, 0), av)
        if paged_prev:
            hc = h0_ref[gi:gi + 1, :]
        hg = av * hc + bv
        groups.append(hg)
        hc = hg[SUBLANES - 1:SUBLANES, :]
    hs = jnp.concatenate(groups, axis=0)
    if paged_prev:
        hl_ref[...] = hs
    else:
        ch_s[...] = groups[-1]
        hl_ref[0] = hs[tm - 1:, :]
    mix_ref[...] = (hs * _gelu(gate_in)).astype(BF16)


def _lru(x2d, g, wg, wx, cw, cb, wri, bri, lam, b, t, tm, prev=None):
    m = x2d.shape[0]
    paged_prev = prev is not None
    tiles_per_seq = max(t // tm, 1)
    rows = lambda n: pl.BlockSpec((tm, n), lambda i: (i, 0))
    in_specs = [rows(D_MODEL), _resident((1, D_MODEL)), _resident((D_MODEL, D_RNN)), _resident((D_MODEL, D_RNN)),
                _resident((LRU_CONV_W, D_RNN)), _resident((1, D_RNN)), _resident((D_RNN, 2 * D_RNN)),
                _resident((1, 2 * D_RNN)), _resident((1, D_RNN))]
    args = [x2d, g, wg, wx, cw, cb, wri, bri, lam]
    scratch = []
    if paged_prev:
        in_specs += [rows(D_RNN)] * 3 + [_resident(prev[3].shape)]
        args += list(prev)
        u_spec, hl_spec = rows(D_RNN), rows(D_RNN)
        u_shape = hl_shape = jax.ShapeDtypeStruct((m, D_RNN), F32)
    else:
        u_spec = pl.BlockSpec((1, LRU_CONV_W - 1, D_RNN), lambda i: (i, 0, 0))
        hl_spec = pl.BlockSpec((1, 1, D_RNN), lambda i: (i, 0, 0))
        u_shape = jax.ShapeDtypeStruct((m // tm, LRU_CONV_W - 1, D_RNN), F32)
        hl_shape = jax.ShapeDtypeStruct((m // tm, 1, D_RNN), F32)
        scratch += [pltpu.VMEM((SUBLANES, D_RNN), F32)] * 2
    return pl.pallas_call(
        functools.partial(_lru_kernel, tm=tm, tiles_per_seq=tiles_per_seq, paged_prev=paged_prev),
        grid=(m // tm,),
        in_specs=in_specs,
        out_specs=(rows(D_RNN), u_spec, hl_spec),
        out_shape=(jax.ShapeDtypeStruct((m, D_RNN), BF16), u_shape, hl_shape),
        scratch_shapes=scratch,
        compiler_params=_params(1),
        name="lru",
    )(*args)


def _rope_tables(pos):
    half = ROT_DIMS // 2
    inv = ROPE_THETA ** (-jnp.arange(half, dtype=F32) / half)
    ang = pos.astype(F32)[:, None] * inv[None, :]
    lane = jnp.arange(LANES) % DH_DIFF
    cos = jnp.cos(ang)[:, lane % half]
    sin = jnp.sin(ang)[:, lane % half]
    cos_t = jnp.where(lane[None, :] < ROT_DIMS, cos, 1.0)
    sin_t = jnp.where(lane[None, :] < half, -sin, jnp.where(lane[None, :] < ROT_DIMS, sin, 0.0))
    return cos_t, sin_t


def _block_diag(w):
    n, c, d = w.shape
    same = jnp.arange(n)[:, None] == jnp.arange(n)[None, :]
    return jnp.where(same[:, None, :, None], w[:, :, None, :], 0).reshape(n * c, n * d)


def _expand_prev(buf, t, k):
    bs, wm1, c = buf.shape
    n = min(k, t)
    return jnp.pad(buf[:, wm1 - k:wm1 - k + n], ((0, 0), (0, t - n), (0, 0))).reshape(bs * t, c)


def kernel(x_prompt, x_sample, cache_fox_k, cache_fox_v, cache_fox_logf, cache_diff_k, cache_diff_v, state_lru_conv, state_lru_h, state_ffn_conv, page_table, mix_norm_g, ab_w_in, ab_b_f, ab_lam_q1, ab_lam_k1, ab_lam_q2, ab_lam_k2, ab_subln_g, ab_w_out, lru_w_gate, lru_w_x, lru_conv_w, lru_conv_b, lru_w_a, lru_b_a, lru_w_i, lru_b_i, lru_lambda, lru_w_out, ffn_norm_g, ffn_w_a, ffn_w_b, ffn_conv_w, ffn_conv_b, ffn_w_down, final_norm_g):
    depth = mix_norm_g.shape[0]
    assert depth == 2 and ab_w_in.shape[0] == 1 and lru_w_x.shape[0] == 1

    w_in = ab_w_in[0]
    o_f = 3 * D_FOX
    w_all = jnp.concatenate([w_in[:, :o_f], w_in[:, o_f + H_FOX:], w_in[:, o_f:o_f + H_FOX],
                             jnp.zeros((D_MODEL, LANES - H_FOX), F32)], axis=1).astype(BF16)
    bf_pad = jnp.pad(ab_b_f[0], (0, LANES - H_FOX)).reshape(1, LANES)
    lamv = jnp.stack([ab_lam_q1[0], ab_lam_k1[0], ab_lam_q2[0], ab_lam_k2[0]]).astype(F32)
    subln_g = ab_subln_g[0].reshape(1, LANES)
    w_out = ab_w_out[0].astype(BF16)
    wg = lru_w_gate[0].astype(BF16)
    wx = lru_w_x[0].astype(BF16)
    wri = jnp.concatenate([_block_diag(lru_w_a[0]), _block_diag(lru_w_i[0])], axis=1).astype(BF16)
    bri = jnp.concatenate([lru_b_a[0], lru_b_i[0]]).reshape(1, 2 * D_RNN)
    lru_wo = lru_w_out[0].astype(BF16)
    ffn_wa = ffn_w_a.astype(BF16)
    ffn_wb = ffn_w_b.astype(BF16)
    ffn_wd = ffn_w_down.astype(BF16)
    lam_init = 0.8 - 0.6 * math.exp(-0.3 * 0)

    def layer0_proj(x2d, pos_rows, tm, seq_len=None):
        cos_t, sin_t = _rope_tables(pos_rows)
        return _inproj(x2d, mix_norm_g[0].reshape(1, D_MODEL), w_all, bf_pad, cos_t, sin_t, tm, seq_len)

    def tail(x2d, mix0, b, t, tm_ffn, tm_lru, tf, ffn_prev, lru_prev):
        x1, nb0 = _ffn(x2d, mix0, w_out, ffn_norm_g[0].reshape(1, D_MODEL), ffn_wa[0], ffn_wb[0],
                       ffn_conv_w[0], ffn_conv_b[0].reshape(1, D_FF), ffn_wd[0], b, t, tm_ffn, tf,
                       prev=ffn_prev[0])
        mix1, u_out, h_out = _lru(x1, mix_norm_g[1].reshape(1, D_MODEL), wg, wx, lru_conv_w[0],
                                  lru_conv_b[0].reshape(1, D_RNN), wri, bri, lru_lambda[0].reshape(1, D_RNN),
                                  b, t, tm_lru, prev=lru_prev)
        y, nb1 = _ffn(x1, (mix1,), lru_wo, ffn_norm_g[1].reshape(1, D_MODEL), ffn_wa[1], ffn_wb[1],
                      ffn_conv_w[1], ffn_conv_b[1].reshape(1, D_FF), ffn_wd[1], b, t, tm_ffn, tf,
                      prev=ffn_prev[1], final_g=final_norm_g.reshape(1, D_MODEL))
        return y, nb0, nb1, u_out, h_out

    bp, tp, _ = x_prompt.shape
    mp = bp * tp
    xp = x_prompt.reshape(mp, D_MODEL)
    (qf, kf, vf, kfb, vfb, logf, logft, qd, kd, vd, kdb, vdb) = layer0_proj(xp, jnp.arange(tp), 512, tp)
    fox_state = lambda a: jnp.transpose(a.reshape(bp, H_FOX, DH_FOX, tp), (0, 3, 1, 2))[None]
    cum = _cumsum_prompt(logft, bp, tp)
    r3 = lambda a: a.reshape(bp, tp, D_FOX)
    tq = 512
    of = _attn_prompt(r3(qf), r3(kfb), r3(vfb), (cum,), True, tq, lam_init)
    od = _attn_prompt(r3(qd), r3(kdb), r3(vdb), (lamv, subln_g), False, tq, lam_init)
    mix0 = (of.reshape(mp, D_FOX), od.reshape(mp, D_DIFF))
    tm_ffn, tm_lru = 256, 256
    y_p, nb0, nb1, u_out, h_out = tail(xp, mix0, bp, tp, tm_ffn, tm_lru, 512, (None, None), None)
    seq_last = lambda a, tm: a[tp // tm - 1::tp // tm]
    nb0, nb1, u_out, h_out = seq_last(nb0, tm_ffn), seq_last(nb1, tm_ffn), seq_last(u_out, tm_lru), seq_last(h_out, tm_lru)
    y_prompt = y_p.reshape(bp, tp, D_MODEL)
    p_state = (fox_state(kf), fox_state(vf),
               logf.reshape(1, bp, tp, H_FOX),
               kd.reshape(1, bp, tp, H_DIFF, 2 * DH_DIFF), vd.reshape(1, bp, tp, H_DIFF, 2 * DH_DIFF),
               u_out[None], h_out.reshape(1, bp, D_RNN), jnp.stack([nb0, nb1]))

    bs, ts, _ = x_sample.shape
    assert ts == SUBLANES
    ms = bs * ts
    n_pages = page_table.shape[1]
    past = n_pages * PAGE_SIZE
    xs = x_sample.reshape(ms, D_MODEL)
    pos_rows = jnp.tile(past + jnp.arange(ts), bs)
    (qf, kf, vf, kfb, vfb, logf, logft, qd, kd, vd, kdb, vdb) = layer0_proj(xs, pos_rows, ms)
    n_pool = cache_fox_k.shape[1]
    logf_pool = jnp.swapaxes(cache_fox_logf[0], 1, 2).reshape(n_pool * H_FOX, PAGE_SIZE)
    lc = _cumsum_rows(logf_pool).reshape(n_pool, H_FOX, PAGE_SIZE)
    logfn = jnp.pad(logft.reshape(H_FOX, bs, ts).transpose(1, 0, 2), ((0, 0), (0, 0), (0, LANES - ts)))
    pad_rows = lambda a: jnp.pad(a.reshape(bs, ts, D_FOX), ((0, 0), (0, PAGE_SIZE - ts), (0, 0)))
    kt_pages = lambda c: jnp.transpose(c, (0, 2, 3, 1)).reshape(n_pool, D_FOX, PAGE_SIZE)
    mix0 = _attn_decode(page_table, qf.astype(F32).reshape(bs, ts, D_FOX), qd.astype(F32).reshape(bs, ts, D_DIFF),
                        pad_rows(kfb), pad_rows(vfb), pad_rows(kdb), pad_rows(vdb),
                        kt_pages(cache_fox_k[0]), kt_pages(cache_fox_v[0]), lc,
                        cache_diff_k[0].reshape(n_pool, PAGE_SIZE * H_DIFF, 2 * DH_DIFF),
                        cache_diff_v[0].reshape(n_pool, PAGE_SIZE * H_DIFF, 2 * DH_DIFF),
                        logfn, lamv, subln_g, 16, lam_init).reshape(ms, D_MIX_AB)
    ffn_prev = tuple((_expand_prev(state_ffn_conv[l], ts, 1), _expand_prev(state_ffn_conv[l], ts, 2))
                     for l in range(depth))
    lru_prev = tuple(_expand_prev(state_lru_conv[0], ts, k) for k in (1, 2, 3)) + (state_lru_h[0],)
    y_s, a0, a1, u_full, h_full = tail(xs, (mix0,), bs, ts, ms, ms, 512, ffn_prev, lru_prev)
    y_sample = y_s.reshape(bs, ts, D_MODEL)
    last = lambda a, k: a.reshape(bs, ts, -1)[:, ts - k:]
    s_state = (kf.reshape(1, bs, ts, H_FOX, DH_FOX), vf.reshape(1, bs, ts, H_FOX, DH_FOX),
               logf.reshape(1, bs, ts, H_FOX),
               kd.reshape(1, bs, ts, H_DIFF, 2 * DH_DIFF), vd.reshape(1, bs, ts, H_DIFF, 2 * DH_DIFF),
               last(u_full, LRU_CONV_W - 1)[None], last(h_full, 1).reshape(1, bs, D_RNN),
               jnp.stack([last(a0, FFN_CONV_W - 1), last(a1, FFN_CONV_W - 1)]))

    return (y_prompt, y_sample) + p_state + s_state
```

```python
import functools
import math

import jax
import jax.numpy as jnp
from jax import lax
from jax.experimental import pallas as pl
from jax.experimental.pallas import tpu as pltpu

F32 = jnp.float32
BF16 = jnp.bfloat16

D_MODEL = 1024
H_FOX = 8
DH_FOX = 64
H_DIFF = 4
DH_DIFF = 64
D_FOX = H_FOX * DH_FOX
D_DIFF = H_DIFF * 2 * DH_DIFF
D_MIX_AB = D_FOX + D_DIFF
ROT_DIMS = DH_DIFF // 4
ROPE_THETA = 500000.0
D_RNN = 1280
N_LRU_BLOCKS = 16
LRU_BW = D_RNN // N_LRU_BLOCKS
LRU_CONV_W = 4
LRU_C = 8.0
D_FF = 3072
FFN_CONV_W = 3
EPS = 1e-6
NEG = -1e30
PAGE_SIZE = 128
LOG2E = math.log2(math.e)

LANES = 128
SUBLANES = 8
VMEM_LIMIT = 56 * 1024 * 1024
N_QKV = 6 * D_FOX
N_INPROJ = N_QKV + LANES


def _params(n_axes):
    return pltpu.CompilerParams(dimension_semantics=("arbitrary",) * n_axes,
                                vmem_limit_bytes=VMEM_LIMIT)


def _full(shape):
    n = len(shape)
    return pl.BlockSpec(shape, lambda *_: (0,) * n)


def _resident(shape):
    n = len(shape)
    return pl.BlockSpec(shape, lambda *_: (0,) * n, pipeline_mode=pl.Buffered(1))


def _rms(x, g):
    return x * lax.rsqrt(jnp.mean(x * x, axis=-1, keepdims=True) + EPS) * g


def _gelu(x):
    c = math.sqrt(2.0 / math.pi)
    return x * (0.5 * (1.0 + jnp.tanh(c * (x + 0.044715 * (x * x * x)))))


def _log_sigmoid(z):
    return -(jnp.maximum(-z, 0.0) + jnp.log1p(jnp.exp(-jnp.abs(z))))


def _sigmoid(z):
    return 0.5 * jnp.tanh(0.5 * z) + 0.5


def _dot(a, b):
    return jnp.dot(a, b, preferred_element_type=F32)


def _dot_nt(a, b):
    return lax.dot_general(a, b, (((1,), (1,)), ((), ())), preferred_element_type=F32)


def _prefix_lanes(x):
    r = lax.broadcasted_iota(jnp.int32, (LANES, LANES), 0)
    c = lax.broadcasted_iota(jnp.int32, (LANES, LANES), 1)
    tri = jnp.where(r <= c, 1.0, 0.0).astype(BF16)
    hi = x.astype(BF16)
    r1 = x - hi.astype(F32)
    mid = r1.astype(BF16)
    lo = (r1 - mid.astype(F32)).astype(BF16)
    return _dot(hi, tri) + _dot(mid, tri) + _dot(lo, tri)


def _inproj_kernel(x_ref, g_ref, w_ref, bf_ref, cos_ref, sin_ref,
                   qf_ref, kf_ref, vf_ref, kfb_ref, vfb_ref, logf_ref, logft_ref,
                   qd_ref, kd_ref, vd_ref, kdb_ref, vdb_ref):
    h = _rms(x_ref[...], g_ref[...]).astype(BF16)

    def proj(i, width=D_FOX):
        return _dot(h, w_ref[:, i * D_FOX:i * D_FOX + width])

    qf_ref[...] = (proj(0) * (DH_FOX ** -0.5 * LOG2E)).astype(BF16)
    def store_state(o_ref, x):
        if len(o_ref.shape) == 3:
            o_ref[0] = x.T
        else:
            o_ref[...] = x

    kf = proj(1)
    store_state(kf_ref, kf)
    kfb_ref[...] = kf.astype(BF16)
    vf = proj(2)
    store_state(vf_ref, vf)
    vfb_ref[...] = vf.astype(BF16)

    logf = _log_sigmoid(proj(6, LANES) + bf_ref[...])
    logf_ref[...] = logf[:, :H_FOX]
    logft_ref[...] = logf.T[:H_FOX, :]

    cosf = jnp.concatenate([cos_ref[...]] * (D_DIFF // LANES), axis=1)
    sinf = jnp.concatenate([sin_ref[...]] * (D_DIFF // LANES), axis=1)
    lane = lax.broadcasted_iota(jnp.int32, cosf.shape, 1) & (DH_DIFF - 1)
    half = ROT_DIMS // 2

    def rope(x):
        partner = jnp.where(lane < half, pltpu.roll(x, D_DIFF - half, 1), pltpu.roll(x, half, 1))
        return jnp.where(lane < ROT_DIMS, x * cosf + partner * sinf, x)

    def store_heads(o_ref, x):
        for hd in range(H_DIFF):
            o_ref[pl.ds(hd, x.shape[0], stride=H_DIFF), :] = x[:, hd * LANES:(hd + 1) * LANES]

    qd_ref[...] = (rope(proj(3)) * (DH_DIFF ** -0.5 * LOG2E)).astype(BF16)
    kd = rope(proj(4))
    store_heads(kd_ref, kd)
    kdb_ref[...] = kd.astype(BF16)
    vd = proj(5)
    store_heads(vd_ref, vd)
    vdb_ref[...] = vd.astype(BF16)


def _inproj(x2d, g, w_all, bf_pad, cos_t, sin_t, tm, seq_len=None):
    m = x2d.shape[0]
    nt = cos_t.shape[0] // tm
    row = lambda n: pl.BlockSpec((tm, n), lambda i: (i, 0))
    tab = pl.BlockSpec((tm, LANES), lambda i: (i % nt, 0))
    b16o = jax.ShapeDtypeStruct((m, D_FOX), BF16)
    if seq_len is None:
        fox_o, fox_spec = jax.ShapeDtypeStruct((m, D_FOX), F32), row(D_FOX)
    else:
        npt = seq_len // tm
        fox_o = jax.ShapeDtypeStruct((m // seq_len, D_FOX, seq_len), F32)
        fox_spec = pl.BlockSpec((1, D_FOX, tm), lambda i: (i // npt, 0, i % npt))
    heads_o = jax.ShapeDtypeStruct((m * H_DIFF, LANES), F32)
    heads_spec = pl.BlockSpec((tm * H_DIFF, LANES), lambda i: (i, 0))
    out_shape = (b16o, fox_o, fox_o, b16o, b16o,
                 jax.ShapeDtypeStruct((m, H_FOX), F32), jax.ShapeDtypeStruct((H_FOX, m), F32),
                 b16o, heads_o, heads_o, b16o, b16o)
    out_specs = ((row(D_FOX), fox_spec, fox_spec, row(D_FOX), row(D_FOX))
                 + (row(H_FOX), pl.BlockSpec((H_FOX, tm), lambda i: (0, i)))
                 + (row(D_FOX), heads_spec, heads_spec, row(D_FOX), row(D_FOX)))
    return pl.pallas_call(
        _inproj_kernel,
        grid=(m // tm,),
        in_specs=[row(D_MODEL), _full((1, D_MODEL)), _full((D_MODEL, N_INPROJ)), _full((1, LANES)), tab, tab],
        out_specs=out_specs,
        out_shape=out_shape,
        compiler_params=_params(1),
        name="inproj",
    )(x2d, g, w_all, bf_pad, cos_t, sin_t)


def _cumsum_prompt_kernel(x_ref, o_ref, *, t):
    carry = jnp.zeros((H_FOX, 1), F32)
    for i in range(t // LANES):
        c = _prefix_lanes(x_ref[:, i * LANES:(i + 1) * LANES]) + carry
        c2 = c * LOG2E
        for j in range(H_FOX // 2):
            o_ref[0, j, :, i * LANES:(i + 1) * LANES] = c2[2 * j:2 * j + 2, :]
        carry = c[:, LANES - 1:LANES]


def _cumsum_prompt(logft, b, t):
    return pl.pallas_call(
        functools.partial(_cumsum_prompt_kernel, t=t),
        grid=(b,),
        in_specs=[pl.BlockSpec((H_FOX, t), lambda i: (0, i))],
        out_specs=pl.BlockSpec((1, H_FOX // 2, 2, t), lambda i: (i, 0, 0, 0)),
        out_shape=jax.ShapeDtypeStruct((b, H_FOX // 2, 2, t), F32),
        compiler_params=_params(1),
        name="cumsum_prompt",
    )(logft)


def _cumsum_rows_kernel(x_ref, o_ref):
    o_ref[...] = _prefix_lanes(x_ref[...])


def _cumsum_rows(x, max_rows=2048):
    rows = x.shape[0]
    tr = max(d for d in range(SUBLANES, max_rows + 1, SUBLANES) if rows % d == 0)
    spec = pl.BlockSpec((tr, LANES), lambda i: (i, 0))
    return pl.pallas_call(
        _cumsum_rows_kernel,
        grid=(rows // tr,),
        in_specs=[spec],
        out_specs=spec,
        out_shape=jax.ShapeDtypeStruct((rows, LANES), F32),
        compiler_params=_params(1),
        name="cumsum_pages",
    )(x)


def _stack_masked(q):
    lane = lax.broadcasted_iota(jnp.int32, q.shape, 1)
    zero = jnp.zeros_like(q)
    return jnp.concatenate([jnp.where(lane < LANES // 2, q, zero),
                            jnp.where(lane >= LANES // 2, q, zero)], axis=0)


def _lambda(lamv, lam_init):
    s1 = jnp.sum(lamv[0:1] * lamv[1:2], axis=1, keepdims=True)
    s2 = jnp.sum(lamv[2:3] * lamv[3:4], axis=1, keepdims=True)
    return jnp.exp(s1) - jnp.exp(s2) + lam_init


def _fox_out(acc, l, r):
    o = acc / l
    lane = lax.broadcasted_iota(jnp.int32, (r, LANES), 1)
    return jnp.where(lane < LANES // 2, o[:r], o[r:])


def _diff_out(acc, l, r, lamv, g, lam_init):
    o = acc / l
    od = o[:r] - _lambda(lamv, lam_init) * o[r:]
    return _rms(od, g) * (1.0 - lam_init)


def _attn_prompt_kernel(*refs, is_fox, tq, rc, lam_init):
    if is_fox:
        q_ref, k_ref, v_ref, c_ref, o_ref = refs[:5]
    else:
        q_ref, k_ref, v_ref, lamv_ref, g_ref, o_ref = refs[:6]
    s0_s, s1_s, r0_s, r1_s, m_s, l_s, acc_s = refs[-7:]
    qi = pl.program_id(2)
    q2 = _stack_masked(q_ref[0])
    nrep = tq // LANES

    def keys(kb):
        return pl.ds(pl.multiple_of(kb * tq, tq), tq)

    def row_max(s):
        return jnp.broadcast_to(jnp.max(s, axis=1, keepdims=True), (2 * tq, LANES))

    def qk(kb, s_s, r_s):
        s = _dot_nt(q2, k_ref[0, keys(kb), :])
        if is_fox:
            c0 = c_ref[0, 0, 0:1, keys(kb)]
            c1 = c_ref[0, 0, 1:2, keys(kb)]
            s = jnp.concatenate([s[:tq] - c0, s[tq:] - c1], axis=0)
        s_s[...] = s
        r_s[...] = row_max(s)

    def softmax_pv(kb, s_s, r_s):
        m_old = m_s[...]
        m_new = jnp.maximum(m_old, r_s[...])
        alpha = jnp.exp2(m_old - m_new)
        m_s[...] = m_new
        sums, probs = [], []
        for ci in range(2 * tq // rc):
            rows = slice(ci * rc, (ci + 1) * rc)
            p = jnp.exp2((s_s[rows, :] - jnp.tile(m_new[rows], (1, nrep))).astype(BF16))
            sums.append(sum(p[:, i * LANES:(i + 1) * LANES] for i in range(nrep)).astype(F32))
            probs.append(p)
        l_s[...] = alpha * l_s[...] + jnp.concatenate(sums, axis=0)
        acc_s[...] = alpha * acc_s[...] + _dot(jnp.concatenate(probs, axis=0), v_ref[0, keys(kb), :])

    m_s[...] = jnp.full(m_s.shape, NEG, F32)
    l_s[...] = jnp.zeros(l_s.shape, F32)
    acc_s[...] = jnp.zeros(acc_s.shape, F32)
    bufs = ((s0_s, r0_s), (s1_s, r1_s))
    qk(0, *bufs[0])

    def body(kb, _):
        for parity in range(2):
            @pl.when((kb & 1) == parity)
            def _():
                qk(kb + 1, *bufs[1 - parity])
                softmax_pv(kb, *bufs[parity])
        return 0

    lax.fori_loop(0, qi, body, 0)
    row = lax.broadcasted_iota(jnp.int32, (2 * tq, tq), 0)
    col = lax.broadcasted_iota(jnp.int32, (2 * tq, tq), 1)
    causal = col <= jnp.where(row >= tq, row - tq, row)
    for parity in range(2):
        @pl.when((qi & 1) == parity)
        def _():
            s_s, r_s = bufs[parity]
            s = jnp.where(causal, s_s[...], NEG)
            s_s[...] = s
            r_s[...] = row_max(s)
            softmax_pv(qi, s_s, r_s)

    acc = acc_s[...]
    l = jnp.sum(l_s[...], axis=1, keepdims=True)
    if is_fox:
        o_ref[0] = _fox_out(acc, l, tq).astype(BF16)
    else:
        o_ref[0] = _diff_out(acc, l, tq, lamv_ref[...], g_ref[...], lam_init).astype(BF16)


def _attn_prompt(q, k, v, extra, is_fox, tq, lam_init, rc=32):
    b, t, _ = q.shape
    nblk = D_FOX // LANES
    wide = pltpu.VMEM((2 * tq, LANES), F32)
    scratch = [pltpu.VMEM((2 * tq, tq), F32), pltpu.VMEM((2 * tq, tq), F32), wide, wide, wide, wide, wide]
    qspec = pl.BlockSpec((1, tq, LANES), lambda bi, j, qi: (bi, qi, j))
    kvspec = pl.BlockSpec((1, t, LANES), lambda bi, j, qi: (bi, 0, j))
    if is_fox:
        especs = [pl.BlockSpec((1, 1, 2, t), lambda bi, j, qi: (bi, j, 0, 0))]
    else:
        especs = [_full((4, DH_DIFF)), _full((1, LANES))]
    return pl.pallas_call(
        functools.partial(_attn_prompt_kernel, is_fox=is_fox, tq=tq, rc=rc, lam_init=lam_init),
        grid=(b, nblk, t // tq),
        in_specs=[qspec, kvspec, kvspec] + especs,
        out_specs=qspec,
        out_shape=jax.ShapeDtypeStruct((b, t, D_FOX), BF16),
        scratch_shapes=scratch,
        compiler_params=_params(3),
        name="attn_prompt_fox" if is_fox else "attn_prompt_diff",
    )(q, k, v, *extra)


def _attn_decode_kernel(pt_ref, *refs, g_pages, n_steps, t_new, lam_init):
    n_in = 6 + 5 * g_pages + 5
    (qf_ref, qd_ref, kfn_ref, vfn_ref, kdn_ref, vdn_ref) = refs[:6]
    pages = refs[6:6 + 5 * g_pages]
    fk_refs, fv_refs, lc_refs, dk_refs, dv_refs = (pages[i * g_pages:(i + 1) * g_pages] for i in range(5))
    logfn_ref, lamv_ref, g_ref = refs[6 + 5 * g_pages:6 + 5 * g_pages + 3]
    o_ref = refs[n_in - 2]
    q2_s, m_s, l_s, acc_s, cc_s = refs[n_in - 1:]
    del pt_ref
    p = pl.program_id(1)
    nfb = D_FOX // LANES
    ndb = D_DIFF // LANES
    r = SUBLANES

    nblk = nfb + ndb
    rb = 2 * r

    def blk(x, j):
        return x[j * rb:(j + 1) * rb]

    def lanes(x, j):
        return x[:, j * LANES:(j + 1) * LANES]

    @pl.when(p == 0)
    def _():
        q = jnp.concatenate([qf_ref[0], qd_ref[0]], axis=1)
        q2_s[...] = jnp.concatenate([_stack_masked(lanes(q, j)) for j in range(nblk)], axis=0).astype(BF16)
        m_s[...] = jnp.full(m_s.shape, NEG, F32)
        l_s[...] = jnp.zeros(l_s.shape, F32)
        acc_s[...] = jnp.zeros(acc_s.shape, F32)
        cc_s[...] = jnp.zeros(cc_s.shape, F32)

    def fox_bias(ck):
        ck = ck * LOG2E
        return jnp.concatenate([jnp.broadcast_to(ck[h:h + 1], (r, ck.shape[1])) for h in range(H_FOX)], axis=0)

    nf = nfb * rb
    nd = ndb * rb

    def update(lo, n, u, pv_fn):
        sl = slice(lo, lo + n)
        m = m_s[sl]
        m_new = jnp.maximum(m, jnp.max(u, axis=1, keepdims=True))
        alpha = jnp.exp2(m - m_new)
        pe = jnp.exp2(u - m_new)
        m_s[sl] = m_new
        l_s[sl] = alpha * l_s[sl] + jnp.sum(pe, axis=1, keepdims=True)
        acc_s[sl] = alpha * acc_s[sl] + pv_fn(pe.astype(BF16))

    def page_step():
        cks = []
        run = cc_s[...]
        for gi in range(g_pages):
            lc = lc_refs[gi][0]
            cks.append(run + lc)
            run = run + lc[:, LANES - 1:LANES]
        cc_s[...] = run
        q2 = q2_s[...]

        def fkv(refs, gi, j):
            return refs[gi][0, j * LANES:(j + 1) * LANES, :].astype(BF16)

        s_fox = jnp.concatenate(
            [jnp.concatenate([_dot(blk(q2, j), fkv(fk_refs, gi, j)) for gi in range(g_pages)], axis=1)
             for j in range(nfb)], axis=0)

        def pv_fox(pb):
            return jnp.concatenate(
                [sum(_dot_nt(lanes(blk(pb, j), gi), fkv(fv_refs, gi, j)) for gi in range(g_pages))
                 for j in range(nfb)], axis=0)

        update(0, nf, s_fox - fox_bias(jnp.concatenate(cks, axis=1)), pv_fox)

        wd = PAGE_SIZE * H_DIFF
        s_diff = jnp.concatenate([_dot_nt(q2[nf:], dk_refs[gi][0].astype(BF16)) for gi in range(g_pages)], axis=1)
        row = lax.broadcasted_iota(jnp.int32, s_diff.shape, 0)
        col = lax.broadcasted_iota(jnp.int32, s_diff.shape, 1)
        own = (col & (H_DIFF - 1)) == row // rb

        def pv_diff(pb):
            return sum(_dot(pb[:, gi * wd:(gi + 1) * wd], dv_refs[gi][0].astype(BF16)) for gi in range(g_pages))

        update(nf, nd, jnp.where(own, s_diff, NEG), pv_diff)

    page_step()

    @pl.when(p == n_steps - 1)
    def _():
        row = lax.broadcasted_iota(jnp.int32, (nf, LANES), 0) & (r - 1)
        col = lax.broadcasted_iota(jnp.int32, (nf, LANES), 1)
        keep = (col <= row) & (col < t_new)
        ck = cc_s[...] + _prefix_lanes(logfn_ref[0])
        q2 = q2_s[...]
        s_fox = jnp.concatenate([_dot_nt(blk(q2, j), lanes(kfn_ref[0], j)) for j in range(nfb)], axis=0)
        update(0, nf, jnp.where(keep, s_fox - fox_bias(ck), NEG),
               lambda pb: jnp.concatenate([_dot(blk(pb, j), lanes(vfn_ref[0], j)) for j in range(nfb)], axis=0))
        s_diff = jnp.concatenate([_dot_nt(blk(q2, nfb + j), lanes(kdn_ref[0], j)) for j in range(ndb)], axis=0)
        update(nf, nd, jnp.where(keep, s_diff, NEG),
               lambda pb: jnp.concatenate([_dot(blk(pb, j), lanes(vdn_ref[0], j)) for j in range(ndb)], axis=0))
        acc = acc_s[...]
        l = l_s[...]
        lamv = lamv_ref[...]
        for j in range(nfb):
            o_ref[0, :, j * LANES:(j + 1) * LANES] = _fox_out(blk(acc, j), blk(l, j), r).astype(BF16)
        for j in range(nfb, nblk):
            o = _diff_out(blk(acc, j), blk(l, j), r, lamv, g_ref[...], lam_init)
            o_ref[0, :, j * LANES:(j + 1) * LANES] = o.astype(BF16)


def _attn_decode(page_table, qf, qd, kfn, vfn, kdn, vdn, fk, fv, lc, dk, dv, logfn, lamv, g, g_pages, lam_init):
    bs, t_new, _ = qf.shape
    n_pages = page_table.shape[1]
    n_steps = n_pages // g_pages
    nblk = (D_FOX + D_DIFF) // LANES

    def seq(shape):
        return pl.BlockSpec((1,) + shape, lambda b, p, pt: (b, 0, 0))

    def paged(shape, gi):
        def index(b, p, pt):
            return (pt[b, p * g_pages + gi],) + (0,) * len(shape)
        return pl.BlockSpec((1,) + shape, index)

    in_specs = [seq((t_new, D_FOX)), seq((t_new, D_DIFF))] + [seq((PAGE_SIZE, D_FOX))] * 4
    args = [qf, qd, kfn, vfn, kdn, vdn]
    for arr in (fk, fv, lc, dk, dv):
        for gi in range(g_pages):
            in_specs.append(paged(arr.shape[1:], gi))
            args.append(arr)
    in_specs += [seq((H_FOX, LANES)),
                 pl.BlockSpec((4, DH_DIFF), lambda b, p, pt: (0, 0)),
                 pl.BlockSpec((1, LANES), lambda b, p, pt: (0, 0))]
    args += [logfn, lamv, g]
    grid_spec = pltpu.PrefetchScalarGridSpec(
        num_scalar_prefetch=1,
        grid=(bs, n_steps),
        in_specs=in_specs,
        out_specs=seq((t_new, D_MIX_AB)),
        scratch_shapes=[pltpu.VMEM((nblk * 2 * t_new, LANES), BF16),
                        pltpu.VMEM((nblk * 2 * t_new, 1), F32),
                        pltpu.VMEM((nblk * 2 * t_new, 1), F32),
                        pltpu.VMEM((nblk * 2 * t_new, LANES), F32),
                        pltpu.VMEM((H_FOX, LANES), F32)],
    )
    return pl.pallas_call(
        functools.partial(_attn_decode_kernel, g_pages=g_pages, n_steps=n_steps, t_new=t_new, lam_init=lam_init),
        grid_spec=grid_spec,
        out_shape=jax.ShapeDtypeStruct((bs, t_new, D_MIX_AB), BF16),
        compiler_params=_params(2),
        name="attn_decode",
    )(page_table, *args)


def _ffn_kernel(*refs, tm, tf, n_mix, tiles_per_seq, paged_prev, final_norm):
    it = iter(refs)
    x_ref = next(it)
    mix_refs = [next(it) for _ in range(n_mix)]
    wo_ref, g_ref, wa_ref, wb_ref, cw_ref, cb_ref, wd_ref = (next(it) for _ in range(7))
    if paged_prev:
        p1_ref, p2_ref = next(it), next(it)
    if final_norm:
        fg_ref = next(it)
    o_ref, nb_ref = next(it), next(it)
    if not paged_prev:
        carry_s = next(it)

        @pl.when((pl.program_id(0) % tiles_per_seq) == 0)
        def _():
            carry_s[...] = jnp.zeros(carry_s.shape, F32)

    y = None
    k0 = 0
    for mix_ref in mix_refs:
        k1 = k0 + mix_ref.shape[1]
        part = _dot(mix_ref[...], wo_ref[k0:k1, :])
        y = part if y is None else part + y
        k0 = k1
    x1 = x_ref[...] + y
    h = _rms(x1, g_ref[...]).astype(BF16)
    nf = D_FF // tf
    row = lax.broadcasted_iota(jnp.int32, (tm if paged_prev else SUBLANES, tf), 0)

    def up(f):
        cols = slice(f * tf, (f + 1) * tf)
        return _dot(h, wa_ref[:, cols]), _dot(h, wb_ref[:, cols])

    def act(f, a, gate):
        cols = slice(f * tf, (f + 1) * tf)
        a1 = pltpu.roll(a, 1, 0)
        a2 = pltpu.roll(a, 2, 0)
        if paged_prev:
            t = row & (SUBLANES - 1)
            a1 = jnp.where(t == 0, p1_ref[:, cols], a1)
            a2 = jnp.where(t < 2, p2_ref[:, cols], a2)
            nb_ref[:, cols] = a
        else:
            c = carry_s[:, cols]
            top1 = jnp.where(row == 0, c[SUBLANES - 1:], a1[:SUBLANES])
            top2 = jnp.where(row == 0, c[SUBLANES - 2:SUBLANES - 1],
                             jnp.where(row == 1, c[SUBLANES - 1:], a2[:SUBLANES]))
            a1 = jnp.concatenate([top1, a1[SUBLANES:]], axis=0)
            a2 = jnp.concatenate([top2, a2[SUBLANES:]], axis=0)
            carry_s[:, cols] = a[tm - SUBLANES:, :]
            nb_ref[0, :, cols] = a[tm - (FFN_CONV_W - 1):, :]
        cw = cw_ref[:, cols]
        ac = cb_ref[:, cols] + a2 * cw[0:1] + a1 * cw[1:2] + a * cw[2:3]
        return (_gelu(ac) * gate).astype(BF16)

    acc = x1
    pending = up(0)
    for f in range(nf):
        following = up(f + 1) if f + 1 < nf else None
        acc = _dot(act(f, *pending), wd_ref[f * tf:(f + 1) * tf, :]) + acc
        pending = following
    if final_norm:
        acc = _rms(acc, fg_ref[...])
    o_ref[...] = acc


def _ffn(x2d, mix_parts, wo, g, wa, wb, cw, cb, wd, b, t, tm, tf, prev=None, final_g=None):
    m = x2d.shape[0]
    paged_prev = prev is not None
    tiles_per_seq = max(t // tm, 1)
    rows = lambda n: pl.BlockSpec((tm, n), lambda i: (i, 0))
    in_specs = ([rows(D_MODEL)] + [rows(p.shape[1]) for p in mix_parts]
                + [_resident(wo.shape), _resident((1, D_MODEL)),
                   _resident((D_MODEL, D_FF)), _resident((D_MODEL, D_FF)), _resident((FFN_CONV_W, D_FF)),
                   _resident((1, D_FF)), _resident((D_FF, D_MODEL))])
    args = [x2d, *mix_parts, wo, g, wa, wb, cw, cb, wd]
    scratch = []
    if paged_prev:
        in_specs += [rows(D_FF)] * 2
        args += list(prev)
        nb_spec = rows(D_FF)
        nb_shape = jax.ShapeDtypeStruct((m, D_FF), F32)
    else:
        nb_spec = pl.BlockSpec((1, FFN_CONV_W - 1, D_FF), lambda i: (i, 0, 0))
        nb_shape = jax.ShapeDtypeStruct((m // tm, FFN_CONV_W - 1, D_FF), F32)
        scratch.append(pltpu.VMEM((SUBLANES, D_FF), F32))
    if final_g is not None:
        in_specs.append(_resident((1, D_MODEL)))
        args.append(final_g)
    return pl.pallas_call(
        functools.partial(_ffn_kernel, tm=tm, tf=tf, n_mix=len(mix_parts), tiles_per_seq=tiles_per_seq,
                          paged_prev=paged_prev, final_norm=final_g is not None),
        grid=(m // tm,),
        in_specs=in_specs,
        out_specs=(rows(D_MODEL), nb_spec),
        out_shape=(jax.ShapeDtypeStruct((m, D_MODEL), F32), nb_shape),
        scratch_shapes=scratch,
        compiler_params=_params(1),
        name="ffn",
    )(*args)


def _lru_kernel(*refs, tm, tiles_per_seq, paged_prev):
    it = iter(refs)
    x_ref, g_ref, wg_ref, wx_ref, cw_ref, cb_ref, wri_ref, bri_ref, lam_ref = (next(it) for _ in range(9))
    if paged_prev:
        p1_ref, p2_ref, p3_ref, h0_ref = (next(it) for _ in range(4))
    mix_ref, u_ref, hl_ref = next(it), next(it), next(it)
    if not paged_prev:
        cu_s, ch_s = next(it), next(it)
    mi = pl.program_id(0)

    h = _rms(x_ref[...], g_ref[...]).astype(BF16)
    u = _dot(h, wx_ref[...])
    gate_in = _dot(h, wg_ref[...])
    u1, u2, u3 = (pltpu.roll(u, k, 0) for k in (1, 2, 3))
    if paged_prev:
        t = lax.broadcasted_iota(jnp.int32, (tm, D_RNN), 0) & (SUBLANES - 1)
        u1 = jnp.where(t == 0, p1_ref[...], u1)
        u2 = jnp.where(t < 2, p2_ref[...], u2)
        u3 = jnp.where(t < 3, p3_ref[...], u3)
        u_ref[...] = u
    else:
        @pl.when((mi % tiles_per_seq) == 0)
        def _():
            cu_s[...] = jnp.zeros(cu_s.shape, F32)
            ch_s[...] = jnp.zeros(ch_s.shape, F32)

        c = cu_s[...]
        row = lax.broadcasted_iota(jnp.int32, (SUBLANES, D_RNN), 0)
        l1, l2, l3 = (c[SUBLANES - k:SUBLANES - k + 1] for k in (1, 2, 3))
        top1 = jnp.where(row == 0, l1, u1[:SUBLANES])
        top2 = jnp.where(row == 0, l2, jnp.where(row == 1, l1, u2[:SUBLANES]))
        top3 = jnp.where(row == 0, l3, jnp.where(row == 1, l2, jnp.where(row == 2, l1, u3[:SUBLANES])))
        u1, u2, u3 = (jnp.concatenate([top, full[SUBLANES:]], axis=0)
                      for top, full in ((top1, u1), (top2, u2), (top3, u3)))
        cu_s[...] = u[tm - SUBLANES:, :]
        u_ref[0] = u[tm - (LRU_CONV_W - 1):, :]
    cw = cw_ref[...]
    xc = cb_ref[...] + u3 * cw[0:1] + u2 * cw[1:2] + u1 * cw[2:3] + u * cw[3:4]

    xb = xc.astype(BF16)
    tile = 2 * LANES
    nt = D_RNN // tile
    near = lambda ct: [kc for kc in (ct - 1, ct, ct + 1) if 0 <= kc < nt]

    def gate_proj(off):
        tiles = []
        for ct in range(nt):
            cols = slice(off + ct * tile, off + (ct + 1) * tile)
            acc = None
            for kc in near(ct):
                part = _dot(xb[:, kc * tile:(kc + 1) * tile], wri_ref[kc * tile:(kc + 1) * tile, cols])
                acc = part if acc is None else part + acc
            tiles.append(bri_ref[:, cols] + acc)
        return jnp.concatenate(tiles, axis=1)

    rg = _sigmoid(gate_proj(0))
    ig = _sigmoid(gate_proj(D_RNN))
    log_a = rg * (LRU_C * _log_sigmoid(lam_ref[...]))
    a_all = jnp.exp(log_a)
    bx_all = jnp.sqrt(-jnp.tanh(log_a) * (a_all * a_all + 1.0)) * (ig * xc)

    r8 = lax.broadcasted_iota(jnp.int32, (SUBLANES, D_RNN), 0)
    hc = None if paged_prev else ch_s[SUBLANES - 1:SUBLANES, :]
    groups = []
    for gi in range(tm // SUBLANES):
        av = a_all[gi * SUBLANES:(gi + 1) * SUBLANES]
        bv = bx_all[gi * SUBLANES:(gi + 1) * SUBLANES]
        for s in (1, 2, 4):
            keep = r8 >= s
            bv = jnp.where(keep, av * pltpu.roll(bv, s, 0) + bv, bv)
            av = jnp.where(keep, av * pltpu.roll(av, s, 0), av)
        if paged_prev:
            hc = h0_ref[gi:gi + 1, :]
        hg = av * hc + bv
        groups.append(hg)
        hc = hg[SUBLANES - 1:SUBLANES, :]
    hs = jnp.concatenate(groups, axis=0)
    if paged_prev:
        hl_ref[...] = hs
    else:
        ch_s[...] = groups[-1]
        hl_ref[0] = hs[tm - 1:, :]
    mix_ref[...] = (hs * _gelu(gate_in)).astype(BF16)


def _lru(x2d, g, wg, wx, cw, cb, wri, bri, lam, b, t, tm, prev=None):
    m = x2d.shape[0]
    paged_prev = prev is not None
    tiles_per_seq = max(t // tm, 1)
    rows = lambda n: pl.BlockSpec((tm, n), lambda i: (i, 0))
    in_specs = [rows(D_MODEL), _resident((1, D_MODEL)), _resident((D_MODEL, D_RNN)), _resident((D_MODEL, D_RNN)),
                _resident((LRU_CONV_W, D_RNN)), _resident((1, D_RNN)), _resident((D_RNN, 2 * D_RNN)),
                _resident((1, 2 * D_RNN)), _resident((1, D_RNN))]
    args = [x2d, g, wg, wx, cw, cb, wri, bri, lam]
    scratch = []
    if paged_prev:
        in_specs += [rows(D_RNN)] * 3 + [_resident(prev[3].shape)]
        args += list(prev)
        u_spec, hl_spec = rows(D_RNN), rows(D_RNN)
        u_shape = hl_shape = jax.ShapeDtypeStruct((m, D_RNN), F32)
    else:
        u_spec = pl.BlockSpec((1, LRU_CONV_W - 1, D_RNN), lambda i: (i, 0, 0))
        hl_spec = pl.BlockSpec((1, 1, D_RNN), lambda i: (i, 0, 0))
        u_shape = jax.ShapeDtypeStruct((m // tm, LRU_CONV_W - 1, D_RNN), F32)
        hl_shape = jax.ShapeDtypeStruct((m // tm, 1, D_RNN), F32)
        scratch += [pltpu.VMEM((SUBLANES, D_RNN), F32)] * 2
    return pl.pallas_call(
        functools.partial(_lru_kernel, tm=tm, tiles_per_seq=tiles_per_seq, paged_prev=paged_prev),
        grid=(m // tm,),
        in_specs=in_specs,
        out_specs=(rows(D_RNN), u_spec, hl_spec),
        out_shape=(jax.ShapeDtypeStruct((m, D_RNN), BF16), u_shape, hl_shape),
        scratch_shapes=scratch,
        compiler_params=_params(1),
        name="lru",
    )(*args)


def _rope_tables(pos):
    half = ROT_DIMS // 2
    inv = ROPE_THETA ** (-jnp.arange(half, dtype=F32) / half)
    ang = pos.astype(F32)[:, None] * inv[None, :]
    lane = jnp.arange(LANES) % DH_DIFF
    cos = jnp.cos(ang)[:, lane % half]
    sin = jnp.sin(ang)[:, lane % half]
    cos_t = jnp.where(lane[None, :] < ROT_DIMS, cos, 1.0)
    sin_t = jnp.where(lane[None, :] < half, -sin, jnp.where(lane[None, :] < ROT_DIMS, sin, 0.0))
    return cos_t, sin_t


def _block_diag(w):
    n, c, d = w.shape
    same = jnp.arange(n)[:, None] == jnp.arange(n)[None, :]
    return jnp.where(same[:, None, :, None], w[:, :, None, :], 0).reshape(n * c, n * d)


def _expand_prev(buf, t, k):
    bs, wm1, c = buf.shape
    n = min(k, t)
    return jnp.pad(buf[:, wm1 - k:wm1 - k + n], ((0, 0), (0, t - n), (0, 0))).reshape(bs * t, c)


def kernel(x_prompt, x_sample, cache_fox_k, cache_fox_v, cache_fox_logf, cache_diff_k, cache_diff_v, state_lru_conv, state_lru_h, state_ffn_conv, page_table, mix_norm_g, ab_w_in, ab_b_f, ab_lam_q1, ab_lam_k1, ab_lam_q2, ab_lam_k2, ab_subln_g, ab_w_out, lru_w_gate, lru_w_x, lru_conv_w, lru_conv_b, lru_w_a, lru_b_a, lru_w_i, lru_b_i, lru_lambda, lru_w_out, ffn_norm_g, ffn_w_a, ffn_w_b, ffn_conv_w, ffn_conv_b, ffn_w_down, final_norm_g):
    depth = mix_norm_g.shape[0]
    assert depth == 2 and ab_w_in.shape[0] == 1 and lru_w_x.shape[0] == 1

    w_in = ab_w_in[0]
    o_f = 3 * D_FOX
    w_all = jnp.concatenate([w_in[:, :o_f], w_in[:, o_f + H_FOX:], w_in[:, o_f:o_f + H_FOX],
                             jnp.zeros((D_MODEL, LANES - H_FOX), F32)], axis=1).astype(BF16)
    bf_pad = jnp.pad(ab_b_f[0], (0, LANES - H_FOX)).reshape(1, LANES)
    lamv = jnp.stack([ab_lam_q1[0], ab_lam_k1[0], ab_lam_q2[0], ab_lam_k2[0]]).astype(F32)
    subln_g = ab_subln_g[0].reshape(1, LANES)
    w_out = ab_w_out[0].astype(BF16)
    wg = lru_w_gate[0].astype(BF16)
    wx = lru_w_x[0].astype(BF16)
    wri = jnp.concatenate([_block_diag(lru_w_a[0]), _block_diag(lru_w_i[0])], axis=1).astype(BF16)
    bri = jnp.concatenate([lru_b_a[0], lru_b_i[0]]).reshape(1, 2 * D_RNN)
    lru_wo = lru_w_out[0].astype(BF16)
    ffn_wa = ffn_w_a.astype(BF16)
    ffn_wb = ffn_w_b.astype(BF16)
    ffn_wd = ffn_w_down.astype(BF16)
    lam_init = 0.8 - 0.6 * math.exp(-0.3 * 0)

    def layer0_proj(x2d, pos_rows, tm, seq_len=None):
        cos_t, sin_t = _rope_tables(pos_rows)
        return _inproj(x2d, mix_norm_g[0].reshape(1, D_MODEL), w_all, bf_pad, cos_t, sin_t, tm, seq_len)

    def tail(x2d, mix0, b, t, tm_ffn, tm_lru, tf, ffn_prev, lru_prev):
        x1, nb0 = _ffn(x2d, mix0, w_out, ffn_norm_g[0].reshape(1, D_MODEL), ffn_wa[0], ffn_wb[0],
                       ffn_conv_w[0], ffn_conv_b[0].reshape(1, D_FF), ffn_wd[0], b, t, tm_ffn, tf,
                       prev=ffn_prev[0])
        mix1, u_out, h_out = _lru(x1, mix_norm_g[1].reshape(1, D_MODEL), wg, wx, lru_conv_w[0],
                                  lru_conv_b[0].reshape(1, D_RNN), wri, bri, lru_lambda[0].reshape(1, D_RNN),
                                  b, t, tm_lru, prev=lru_prev)
        y, nb1 = _ffn(x1, (mix1,), lru_wo, ffn_norm_g[1].reshape(1, D_MODEL), ffn_wa[1], ffn_wb[1],
                      ffn_conv_w[1], ffn_conv_b[1].reshape(1, D_FF), ffn_wd[1], b, t, tm_ffn, tf,
                      prev=ffn_prev[1], final_g=final_norm_g.reshape(1, D_MODEL))
        return y, nb0, nb1, u_out, h_out

    bp, tp, _ = x_prompt.shape
    mp = bp * tp
    xp = x_prompt.reshape(mp, D_MODEL)
    (qf, kf, vf, kfb, vfb, logf, logft, qd, kd, vd, kdb, vdb) = layer0_proj(xp, jnp.arange(tp), 512, tp)
    fox_state = lambda a: jnp.transpose(a.reshape(bp, H_FOX, DH_FOX, tp), (0, 3, 1, 2))[None]
    cum = _cumsum_prompt(logft, bp, tp)
    r3 = lambda a: a.reshape(bp, tp, D_FOX)
    tq = 512
    of = _attn_prompt(r3(qf), r3(kfb), r3(vfb), (cum,), True, tq, lam_init)
    od = _attn_prompt(r3(qd), r3(kdb), r3(vdb), (lamv, subln_g), False, tq, lam_init)
    mix0 = (of.reshape(mp, D_FOX), od.reshape(mp, D_DIFF))
    tm_ffn, tm_lru = 256, 256
    y_p, nb0, nb1, u_out, h_out = tail(xp, mix0, bp, tp, tm_ffn, tm_lru, 512, (None, None), None)
    seq_last = lambda a, tm: a[tp // tm - 1::tp // tm]
    nb0, nb1, u_out, h_out = seq_last(nb0, tm_ffn), seq_last(nb1, tm_ffn), seq_last(u_out, tm_lru), seq_last(h_out, tm_lru)
    y_prompt = y_p.reshape(bp, tp, D_MODEL)
    p_state = (fox_state(kf), fox_state(vf),
               logf.reshape(1, bp, tp, H_FOX),
               kd.reshape(1, bp, tp, H_DIFF, 2 * DH_DIFF), vd.reshape(1, bp, tp, H_DIFF, 2 * DH_DIFF),
               u_out[None], h_out.reshape(1, bp, D_RNN), jnp.stack([nb0, nb1]))

    bs, ts, _ = x_sample.shape
    assert ts == SUBLANES
    ms = bs * ts
    n_pages = page_table.shape[1]
    past = n_pages * PAGE_SIZE
    xs = x_sample.reshape(ms, D_MODEL)
    pos_rows = jnp.tile(past + jnp.arange(ts), bs)
    (qf, kf, vf, kfb, vfb, logf, logft, qd, kd, vd, kdb, vdb) = layer0_proj(xs, pos_rows, ms)
    n_pool = cache_fox_k.shape[1]
    logf_pool = jnp.swapaxes(cache_fox_logf[0], 1, 2).reshape(n_pool * H_FOX, PAGE_SIZE)
    lc = _cumsum_rows(logf_pool).reshape(n_pool, H_FOX, PAGE_SIZE)
    logfn = jnp.pad(logft.reshape(H_FOX, bs, ts).transpose(1, 0, 2), ((0, 0), (0, 0), (0, LANES - ts)))
    pad_rows = lambda a: jnp.pad(a.reshape(bs, ts, D_FOX), ((0, 0), (0, PAGE_SIZE - ts), (0, 0)))
    kt_pages = lambda c: jnp.transpose(c, (0, 2, 3, 1)).reshape(n_pool, D_FOX, PAGE_SIZE)
    mix0 = _attn_decode(page_table, qf.astype(F32).reshape(bs, ts, D_FOX), qd.astype(F32).reshape(bs, ts, D_DIFF),
                        pad_rows(kfb), pad_rows(vfb), pad_rows(kdb), pad_rows(vdb),
                        kt_pages(cache_fox_k[0]), kt_pages(cache_fox_v[0]), lc,
                        cache_diff_k[0].reshape(n_pool, PAGE_SIZE * H_DIFF, 2 * DH_DIFF),
                        cache_diff_v[0].reshape(n_pool, PAGE_SIZE * H_DIFF, 2 * DH_DIFF),
                        logfn, lamv, subln_g, 16, lam_init).reshape(ms, D_MIX_AB)
    ffn_prev = tuple((_expand_prev(state_ffn_conv[l], ts, 1), _expand_prev(state_ffn_conv[l], ts, 2))
                     for l in range(depth))
    lru_prev = tuple(_expand_prev(state_lru_conv[0], ts, k) for k in (1, 2, 3)) + (state_lru_h[0],)
    y_s, a0, a1, u_full, h_full = tail(xs, (mix0,), bs, ts, ms, ms, 512, ffn_prev, lru_prev)
    y_sample = y_s.reshape(bs, ts, D_MODEL)
    last = lambda a, k: a.reshape(bs, ts, -1)[:, ts - k:]
    s_state = (kf.reshape(1, bs, ts, H_FOX, DH_FOX), vf.reshape(1, bs, ts, H_FOX, DH_FOX),
               logf.reshape(1, bs, ts, H_FOX),
               kd.reshape(1, bs, ts, H_DIFF, 2 * DH_DIFF), vd.reshape(1, bs, ts, H_DIFF, 2 * DH_DIFF),
               last(u_full, LRU_CONV_W - 1)[None], last(h_full, 1).reshape(1, bs, D_RNN),
               jnp.stack([last(a0, FFN_CONV_W - 1), last(a1, FFN_CONV_W - 1)]))

    return (y_prompt, y_sample) + p_state + s_state
```

```python
import functools
import math

import jax
import jax.numpy as jnp
from jax import lax
from jax.experimental import pallas as pl
from jax.experimental.pallas import tpu as pltpu

F32 = jnp.float32
BF16 = jnp.bfloat16

D_MODEL = 1024
H_FOX = 8
DH_FOX = 64
H_DIFF = 4
DH_DIFF = 64
D_FOX = H_FOX * DH_FOX
D_DIFF = H_DIFF * 2 * DH_DIFF
D_MIX_AB = D_FOX + D_DIFF
ROT_DIMS = DH_DIFF // 4
ROPE_THETA = 500000.0
D_RNN = 1280
N_LRU_BLOCKS = 16
LRU_BW = D_RNN // N_LRU_BLOCKS
LRU_CONV_W = 4
LRU_C = 8.0
D_FF = 3072
FFN_CONV_W = 3
EPS = 1e-6
NEG = -1e30
PAGE_SIZE = 128
LOG2E = math.log2(math.e)

LANES = 128
SUBLANES = 8
VMEM_LIMIT = 56 * 1024 * 1024
N_QKV = 6 * D_FOX
N_INPROJ = N_QKV + LANES


def _params(n_axes):
    return pltpu.CompilerParams(dimension_semantics=("arbitrary",) * n_axes,
                                vmem_limit_bytes=VMEM_LIMIT)


def _full(shape):
    n = len(shape)
    return pl.BlockSpec(shape, lambda *_: (0,) * n)


def _resident(shape):
    n = len(shape)
    return pl.BlockSpec(shape, lambda *_: (0,) * n, pipeline_mode=pl.Buffered(1))


def _rms(x, g):
    return x * lax.rsqrt(jnp.mean(x * x, axis=-1, keepdims=True) + EPS) * g


def _gelu(x):
    c = math.sqrt(2.0 / math.pi)
    return x * (0.5 * (1.0 + jnp.tanh(c * (x + 0.044715 * (x * x * x)))))


def _log_sigmoid(z):
    return -(jnp.maximum(-z, 0.0) + jnp.log1p(jnp.exp(-jnp.abs(z))))


def _sigmoid(z):
    return 0.5 * jnp.tanh(0.5 * z) + 0.5


def _dot(a, b):
    return jnp.dot(a, b, preferred_element_type=F32)


def _dot_nt(a, b):
    return lax.dot_general(a, b, (((1,), (1,)), ((), ())), preferred_element_type=F32)


def _prefix_lanes(x):
    r = lax.broadcasted_iota(jnp.int32, (LANES, LANES), 0)
    c = lax.broadcasted_iota(jnp.int32, (LANES, LANES), 1)
    tri = jnp.where(r <= c, 1.0, 0.0).astype(BF16)
    hi = x.astype(BF16)
    r1 = x - hi.astype(F32)
    mid = r1.astype(BF16)
    lo = (r1 - mid.astype(F32)).astype(BF16)
    return _dot(hi, tri) + _dot(mid, tri) + _dot(lo, tri)


def _inproj_kernel(x_ref, g_ref, w_ref, bf_ref, cos_ref, sin_ref,
                   qf_ref, kf_ref, vf_ref, kfb_ref, vfb_ref, logft_ref,
                   qd_ref, kd_ref, vd_ref, kdb_ref, vdb_ref):
    h = _rms(x_ref[...], g_ref[...]).astype(BF16)

    def proj(i, width=D_FOX):
        return _dot(h, w_ref[:, i * D_FOX:i * D_FOX + width])

    qf_ref[...] = (proj(0) * (DH_FOX ** -0.5 * LOG2E)).astype(BF16)
    def store_state(o_ref, x):
        if len(o_ref.shape) == 3:
            o_ref[0] = x.T
        else:
            o_ref[...] = x

    kf = proj(1)
    store_state(kf_ref, kf)
    kfb_ref[...] = kf.astype(BF16)
    vf = proj(2)
    store_state(vf_ref, vf)
    vfb_ref[...] = vf.astype(BF16)

    logf = _log_sigmoid(proj(6, LANES) + bf_ref[...])
    logft_ref[...] = logf.T[:H_FOX, :]

    cosf = jnp.concatenate([cos_ref[...]] * (D_DIFF // LANES), axis=1)
    sinf = jnp.concatenate([sin_ref[...]] * (D_DIFF // LANES), axis=1)
    lane = lax.broadcasted_iota(jnp.int32, cosf.shape, 1) & (DH_DIFF - 1)
    half = ROT_DIMS // 2

    def rope(x):
        partner = jnp.where(lane < half, pltpu.roll(x, D_DIFF - half, 1), pltpu.roll(x, half, 1))
        return jnp.where(lane < ROT_DIMS, x * cosf + partner * sinf, x)

    def store_heads(o_ref, x):
        for hd in range(H_DIFF):
            o_ref[pl.ds(hd, x.shape[0], stride=H_DIFF), :] = x[:, hd * LANES:(hd + 1) * LANES]

    qd_ref[...] = (rope(proj(3)) * (DH_DIFF ** -0.5 * LOG2E)).astype(BF16)
    kd = rope(proj(4))
    store_heads(kd_ref, kd)
    kdb_ref[...] = kd.astype(BF16)
    vd = proj(5)
    store_heads(vd_ref, vd)
    vdb_ref[...] = vd.astype(BF16)


def _inproj(x2d, g, w_all, bf_pad, cos_t, sin_t, tm, seq_len=None):
    m = x2d.shape[0]
    nt = cos_t.shape[0] // tm
    row = lambda n: pl.BlockSpec((tm, n), lambda i: (i, 0))
    tab = pl.BlockSpec((tm, LANES), lambda i: (i % nt, 0))
    b16o = jax.ShapeDtypeStruct((m, D_FOX), BF16)
    if seq_len is None:
        fox_o, fox_spec = jax.ShapeDtypeStruct((m, D_FOX), F32), row(D_FOX)
    else:
        npt = seq_len // tm
        fox_o = jax.ShapeDtypeStruct((m // seq_len, D_FOX, seq_len), F32)
        fox_spec = pl.BlockSpec((1, D_FOX, tm), lambda i: (i // npt, 0, i % npt))
    heads_o = jax.ShapeDtypeStruct((m * H_DIFF, LANES), F32)
    heads_spec = pl.BlockSpec((tm * H_DIFF, LANES), lambda i: (i, 0))
    out_shape = (b16o, fox_o, fox_o, b16o, b16o,
                 jax.ShapeDtypeStruct((H_FOX, m), F32),
                 b16o, heads_o, heads_o, b16o, b16o)
    out_specs = ((row(D_FOX), fox_spec, fox_spec, row(D_FOX), row(D_FOX))
                 + (pl.BlockSpec((H_FOX, tm), lambda i: (0, i)),)
                 + (row(D_FOX), heads_spec, heads_spec, row(D_FOX), row(D_FOX)))
    return pl.pallas_call(
        _inproj_kernel,
        grid=(m // tm,),
        in_specs=[row(D_MODEL), _full((1, D_MODEL)), _full((D_MODEL, N_INPROJ)), _full((1, LANES)), tab, tab],
        out_specs=out_specs,
        out_shape=out_shape,
        compiler_params=_params(1),
        name="inproj",
    )(x2d, g, w_all, bf_pad, cos_t, sin_t)


def _cumsum_prompt_kernel(x_ref, o_ref, *, t):
    carry = jnp.zeros((H_FOX, 1), F32)
    for i in range(t // LANES):
        c = _prefix_lanes(x_ref[:, i * LANES:(i + 1) * LANES]) + carry
        c2 = c * LOG2E
        for j in range(H_FOX // 2):
            o_ref[0, j, :, i * LANES:(i + 1) * LANES] = c2[2 * j:2 * j + 2, :]
        carry = c[:, LANES - 1:LANES]


def _cumsum_prompt(logft, b, t):
    return pl.pallas_call(
        functools.partial(_cumsum_prompt_kernel, t=t),
        grid=(b,),
        in_specs=[pl.BlockSpec((H_FOX, t), lambda i: (0, i))],
        out_specs=pl.BlockSpec((1, H_FOX // 2, 2, t), lambda i: (i, 0, 0, 0)),
        out_shape=jax.ShapeDtypeStruct((b, H_FOX // 2, 2, t), F32),
        compiler_params=_params(1),
        name="cumsum_prompt",
    )(logft)


def _cumsum_rows_kernel(x_ref, o_ref):
    o_ref[...] = _prefix_lanes(x_ref[...])


def _cumsum_rows(x, max_rows=2048):
    rows = x.shape[0]
    tr = max(d for d in range(SUBLANES, max_rows + 1, SUBLANES) if rows % d == 0)
    spec = pl.BlockSpec((tr, LANES), lambda i: (i, 0))
    return pl.pallas_call(
        _cumsum_rows_kernel,
        grid=(rows // tr,),
        in_specs=[spec],
        out_specs=spec,
        out_shape=jax.ShapeDtypeStruct((rows, LANES), F32),
        compiler_params=_params(1),
        name="cumsum_pages",
    )(x)


def _stack_masked(q):
    lane = lax.broadcasted_iota(jnp.int32, q.shape, 1)
    zero = jnp.zeros_like(q)
    return jnp.concatenate([jnp.where(lane < LANES // 2, q, zero),
                            jnp.where(lane >= LANES // 2, q, zero)], axis=0)


def _lambda(lamv, lam_init):
    s1 = jnp.sum(lamv[0:1] * lamv[1:2], axis=1, keepdims=True)
    s2 = jnp.sum(lamv[2:3] * lamv[3:4], axis=1, keepdims=True)
    return jnp.exp(s1) - jnp.exp(s2) + lam_init


def _fox_out(acc, l, r):
    o = acc / l
    lane = lax.broadcasted_iota(jnp.int32, (r, LANES), 1)
    return jnp.where(lane < LANES // 2, o[:r], o[r:])


def _diff_out(acc, l, r, lamv, g, lam_init):
    o = acc / l
    od = o[:r] - _lambda(lamv, lam_init) * o[r:]
    return _rms(od, g) * (1.0 - lam_init)


def _attn_prompt_kernel(*refs, is_fox, tq, rc, lam_init):
    if is_fox:
        q_ref, k_ref, v_ref, c_ref, o_ref = refs[:5]
    else:
        q_ref, k_ref, v_ref, lamv_ref, g_ref, o_ref = refs[:6]
    s0_s, s1_s, r0_s, r1_s, m_s, l_s, acc_s = refs[-7:]
    qi = pl.program_id(2)
    q2 = _stack_masked(q_ref[0])
    nrep = tq // LANES

    def keys(kb):
        return pl.ds(pl.multiple_of(kb * tq, tq), tq)

    def row_max(s):
        return jnp.broadcast_to(jnp.max(s, axis=1, keepdims=True), (2 * tq, LANES))

    def qk(kb, s_s, r_s):
        s = _dot_nt(q2, k_ref[0, keys(kb), :])
        if is_fox:
            c0 = c_ref[0, 0, 0:1, keys(kb)]
            c1 = c_ref[0, 0, 1:2, keys(kb)]
            s = jnp.concatenate([s[:tq] - c0, s[tq:] - c1], axis=0)
        s_s[...] = s
        r_s[...] = row_max(s)

    def softmax_pv(kb, s_s, r_s):
        m_old = m_s[...]
        m_new = jnp.maximum(m_old, r_s[...])
        alpha = jnp.exp2(m_old - m_new)
        m_s[...] = m_new
        sums, probs = [], []
        for ci in range(2 * tq // rc):
            rows = slice(ci * rc, (ci + 1) * rc)
            p = jnp.exp2((s_s[rows, :] - jnp.tile(m_new[rows], (1, nrep))).astype(BF16))
            sums.append(sum(p[:, i * LANES:(i + 1) * LANES] for i in range(nrep)).astype(F32))
            probs.append(p)
        l_s[...] = alpha * l_s[...] + jnp.concatenate(sums, axis=0)
        acc_s[...] = alpha * acc_s[...] + _dot(jnp.concatenate(probs, axis=0), v_ref[0, keys(kb), :])

    m_s[...] = jnp.full(m_s.shape, NEG, F32)
    l_s[...] = jnp.zeros(l_s.shape, F32)
    acc_s[...] = jnp.zeros(acc_s.shape, F32)
    bufs = ((s0_s, r0_s), (s1_s, r1_s))
    qk(0, *bufs[0])

    def body(kb, _):
        for parity in range(2):
            @pl.when((kb & 1) == parity)
            def _():
                qk(kb + 1, *bufs[1 - parity])
                softmax_pv(kb, *bufs[parity])
        return 0

    lax.fori_loop(0, qi, body, 0)
    row = lax.broadcasted_iota(jnp.int32, (2 * tq, tq), 0)
    col = lax.broadcasted_iota(jnp.int32, (2 * tq, tq), 1)
    causal = col <= jnp.where(row >= tq, row - tq, row)
    for parity in range(2):
        @pl.when((qi & 1) == parity)
        def _():
            s_s, r_s = bufs[parity]
            s = jnp.where(causal, s_s[...], NEG)
            s_s[...] = s
            r_s[...] = row_max(s)
            softmax_pv(qi, s_s, r_s)

    acc = acc_s[...]
    l = jnp.sum(l_s[...], axis=1, keepdims=True)
    if is_fox:
        o_ref[0] = _fox_out(acc, l, tq).astype(BF16)
    else:
        o_ref[0] = _diff_out(acc, l, tq, lamv_ref[...], g_ref[...], lam_init).astype(BF16)


def _attn_prompt(q, k, v, extra, is_fox, tq, lam_init, rc=32):
    b, t, _ = q.shape
    nblk = D_FOX // LANES
    wide = pltpu.VMEM((2 * tq, LANES), F32)
    scratch = [pltpu.VMEM((2 * tq, tq), F32), pltpu.VMEM((2 * tq, tq), F32), wide, wide, wide, wide, wide]
    qspec = pl.BlockSpec((1, tq, LANES), lambda bi, j, qi: (bi, qi, j))
    kvspec = pl.BlockSpec((1, t, LANES), lambda bi, j, qi: (bi, 0, j))
    if is_fox:
        especs = [pl.BlockSpec((1, 1, 2, t), lambda bi, j, qi: (bi, j, 0, 0))]
    else:
        especs = [_full((4, DH_DIFF)), _full((1, LANES))]
    return pl.pallas_call(
        functools.partial(_attn_prompt_kernel, is_fox=is_fox, tq=tq, rc=rc, lam_init=lam_init),
        grid=(b, nblk, t // tq),
        in_specs=[qspec, kvspec, kvspec] + especs,
        out_specs=qspec,
        out_shape=jax.ShapeDtypeStruct((b, t, D_FOX), BF16),
        scratch_shapes=scratch,
        compiler_params=_params(3),
        name="attn_prompt_fox" if is_fox else "attn_prompt_diff",
    )(q, k, v, *extra)


def _attn_decode_kernel(pt_ref, *refs, g_pages, n_steps, t_new, lam_init):
    n_in = 6 + 5 * g_pages + 5
    (qf_ref, qd_ref, kfn_ref, vfn_ref, kdn_ref, vdn_ref) = refs[:6]
    pages = refs[6:6 + 5 * g_pages]
    fk_refs, fv_refs, lc_refs, dk_refs, dv_refs = (pages[i * g_pages:(i + 1) * g_pages] for i in range(5))
    logfn_ref, lamv_ref, g_ref = refs[6 + 5 * g_pages:6 + 5 * g_pages + 3]
    o_ref = refs[n_in - 2]
    q2_s, m_s, l_s, acc_s, cc_s = refs[n_in - 1:]
    del pt_ref
    p = pl.program_id(1)
    nfb = D_FOX // LANES
    ndb = D_DIFF // LANES
    r = SUBLANES

    nblk = nfb + ndb
    rb = 2 * r

    def blk(x, j):
        return x[j * rb:(j + 1) * rb]

    def lanes(x, j):
        return x[:, j * LANES:(j + 1) * LANES]

    @pl.when(p == 0)
    def _():
        q = jnp.concatenate([qf_ref[0], qd_ref[0]], axis=1)
        q2_s[...] = jnp.concatenate([_stack_masked(lanes(q, j)) for j in range(nblk)], axis=0).astype(BF16)
        m_s[...] = jnp.full(m_s.shape, NEG, F32)
        l_s[...] = jnp.zeros(l_s.shape, F32)
        acc_s[...] = jnp.zeros(acc_s.shape, F32)
        cc_s[...] = jnp.zeros(cc_s.shape, F32)

    def fox_bias(ck):
        ck = ck * LOG2E
        return jnp.concatenate([jnp.broadcast_to(ck[h:h + 1], (r, ck.shape[1])) for h in range(H_FOX)], axis=0)

    nf = nfb * rb
    nd = ndb * rb

    def update(lo, n, u, pv_fn):
        sl = slice(lo, lo + n)
        m = m_s[sl]
        m_new = jnp.maximum(m, jnp.max(u, axis=1, keepdims=True))
        alpha = jnp.exp2(m - m_new)
        pe = jnp.exp2(u - m_new)
        m_s[sl] = m_new
        l_s[sl] = alpha * l_s[sl] + jnp.sum(pe, axis=1, keepdims=True)
        acc_s[sl] = alpha * acc_s[sl] + pv_fn(pe.astype(BF16))

    def page_step():
        cks = []
        run = cc_s[...]
        for gi in range(g_pages):
            lc = lc_refs[gi][0]
            cks.append(run + lc)
            run = run + lc[:, LANES - 1:LANES]
        cc_s[...] = run
        q2 = q2_s[...]

        def fkv(refs, gi, j):
            return refs[gi][0, j * LANES:(j + 1) * LANES, :].astype(BF16)

        s_fox = jnp.concatenate(
            [jnp.concatenate([_dot(blk(q2, j), fkv(fk_refs, gi, j)) for gi in range(g_pages)], axis=1)
             for j in range(nfb)], axis=0)

        def pv_fox(pb):
            return jnp.concatenate(
                [sum(_dot_nt(lanes(blk(pb, j), gi), fkv(fv_refs, gi, j)) for gi in range(g_pages))
                 for j in range(nfb)], axis=0)

        update(0, nf, s_fox - fox_bias(jnp.concatenate(cks, axis=1)), pv_fox)

        wd = PAGE_SIZE * H_DIFF
        s_diff = jnp.concatenate([_dot_nt(q2[nf:], dk_refs[gi][0].astype(BF16)) for gi in range(g_pages)], axis=1)
        row = lax.broadcasted_iota(jnp.int32, s_diff.shape, 0)
        col = lax.broadcasted_iota(jnp.int32, s_diff.shape, 1)
        own = (col & (H_DIFF - 1)) == row // rb

        def pv_diff(pb):
            return sum(_dot(pb[:, gi * wd:(gi + 1) * wd], dv_refs[gi][0].astype(BF16)) for gi in range(g_pages))

        update(nf, nd, jnp.where(own, s_diff, NEG), pv_diff)

    page_step()

    @pl.when(p == n_steps - 1)
    def _():
        row = lax.broadcasted_iota(jnp.int32, (nf, LANES), 0) & (r - 1)
        col = lax.broadcasted_iota(jnp.int32, (nf, LANES), 1)
        keep = (col <= row) & (col < t_new)
        ck = cc_s[...] + _prefix_lanes(logfn_ref[0])
        q2 = q2_s[...]
        s_fox = jnp.concatenate([_dot_nt(blk(q2, j), lanes(kfn_ref[0], j)) for j in range(nfb)], axis=0)
        update(0, nf, jnp.where(keep, s_fox - fox_bias(ck), NEG),
               lambda pb: jnp.concatenate([_dot(blk(pb, j), lanes(vfn_ref[0], j)) for j in range(nfb)], axis=0))
        s_diff = jnp.concatenate([_dot_nt(blk(q2, nfb + j), lanes(kdn_ref[0], j)) for j in range(ndb)], axis=0)
        update(nf, nd, jnp.where(keep, s_diff, NEG),
               lambda pb: jnp.concatenate([_dot(blk(pb, j), lanes(vdn_ref[0], j)) for j in range(ndb)], axis=0))
        acc = acc_s[...]
        l = l_s[...]
        lamv = lamv_ref[...]
        for j in range(nfb):
            o_ref[0, :, j * LANES:(j + 1) * LANES] = _fox_out(blk(acc, j), blk(l, j), r).astype(BF16)
        for j in range(nfb, nblk):
            o = _diff_out(blk(acc, j), blk(l, j), r, lamv, g_ref[...], lam_init)
            o_ref[0, :, j * LANES:(j + 1) * LANES] = o.astype(BF16)


def _attn_decode(page_table, qf, qd, kfn, vfn, kdn, vdn, fk, fv, lc, dk, dv, logfn, lamv, g, g_pages, lam_init):
    bs, t_new, _ = qf.shape
    n_pages = page_table.shape[1]
    n_steps = n_pages // g_pages
    nblk = (D_FOX + D_DIFF) // LANES

    def seq(shape):
        return pl.BlockSpec((1,) + shape, lambda b, p, pt: (b, 0, 0))

    def paged(shape, gi):
        def index(b, p, pt):
            return (pt[b, p * g_pages + gi],) + (0,) * len(shape)
        return pl.BlockSpec((1,) + shape, index)

    in_specs = [seq((t_new, D_FOX)), seq((t_new, D_DIFF))] + [seq((PAGE_SIZE, D_FOX))] * 4
    args = [qf, qd, kfn, vfn, kdn, vdn]
    for arr in (fk, fv, lc, dk, dv):
        for gi in range(g_pages):
            in_specs.append(paged(arr.shape[1:], gi))
            args.append(arr)
    in_specs += [seq((H_FOX, LANES)),
                 pl.BlockSpec((4, DH_DIFF), lambda b, p, pt: (0, 0)),
                 pl.BlockSpec((1, LANES), lambda b, p, pt: (0, 0))]
    args += [logfn, lamv, g]
    grid_spec = pltpu.PrefetchScalarGridSpec(
        num_scalar_prefetch=1,
        grid=(bs, n_steps),
        in_specs=in_specs,
        out_specs=seq((t_new, D_MIX_AB)),
        scratch_shapes=[pltpu.VMEM((nblk * 2 * t_new, LANES), BF16),
                        pltpu.VMEM((nblk * 2 * t_new, 1), F32),
                        pltpu.VMEM((nblk * 2 * t_new, 1), F32),
                        pltpu.VMEM((nblk * 2 * t_new, LANES), F32),
                        pltpu.VMEM((H_FOX, LANES), F32)],
    )
    return pl.pallas_call(
        functools.partial(_attn_decode_kernel, g_pages=g_pages, n_steps=n_steps, t_new=t_new, lam_init=lam_init),
        grid_spec=grid_spec,
        out_shape=jax.ShapeDtypeStruct((bs, t_new, D_MIX_AB), BF16),
        compiler_params=_params(2),
        name="attn_decode",
    )(page_table, *args)


def _ffn_kernel(*refs, tm, tf, n_mix, tiles_per_seq, paged_prev, final_norm):
    it = iter(refs)
    x_ref = next(it)
    mix_refs = [next(it) for _ in range(n_mix)]
    wo_ref, g_ref, wa_ref, wb_ref, cw_ref, cb_ref, wd_ref = (next(it) for _ in range(7))
    if paged_prev:
        p1_ref, p2_ref = next(it), next(it)
    if final_norm:
        fg_ref = next(it)
    o_ref, nb_ref = next(it), next(it)
    if not paged_prev:
        carry_s = next(it)

        @pl.when((pl.program_id(0) % tiles_per_seq) == 0)
        def _():
            carry_s[...] = jnp.zeros(carry_s.shape, F32)

    y = None
    k0 = 0
    for mix_ref in mix_refs:
        k1 = k0 + mix_ref.shape[1]
        part = _dot(mix_ref[...], wo_ref[k0:k1, :])
        y = part if y is None else part + y
        k0 = k1
    x1 = x_ref[...] + y
    h = _rms(x1, g_ref[...]).astype(BF16)
    nf = D_FF // tf
    row = lax.broadcasted_iota(jnp.int32, (tm if paged_prev else SUBLANES, tf), 0)

    def up(f):
        cols = slice(f * tf, (f + 1) * tf)
        return _dot(h, wa_ref[:, cols]), _dot(h, wb_ref[:, cols])

    def act(f, a, gate):
        cols = slice(f * tf, (f + 1) * tf)
        a1 = pltpu.roll(a, 1, 0)
        a2 = pltpu.roll(a, 2, 0)
        if paged_prev:
            t = row & (SUBLANES - 1)
            a1 = jnp.where(t == 0, p1_ref[:, cols], a1)
            a2 = jnp.where(t < 2, p2_ref[:, cols], a2)
            nb_ref[:, cols] = a
        else:
            c = carry_s[:, cols]
            top1 = jnp.where(row == 0, c[SUBLANES - 1:], a1[:SUBLANES])
            top2 = jnp.where(row == 0, c[SUBLANES - 2:SUBLANES - 1],
                             jnp.where(row == 1, c[SUBLANES - 1:], a2[:SUBLANES]))
            a1 = jnp.concatenate([top1, a1[SUBLANES:]], axis=0)
            a2 = jnp.concatenate([top2, a2[SUBLANES:]], axis=0)
            carry_s[:, cols] = a[tm - SUBLANES:, :]
            nb_ref[0, :, cols] = a[tm - (FFN_CONV_W - 1):, :]
        cw = cw_ref[:, cols]
        ac = cb_ref[:, cols] + a2 * cw[0:1] + a1 * cw[1:2] + a * cw[2:3]
        return (_gelu(ac) * gate).astype(BF16)

    acc = x1
    pending = up(0)
    for f in range(nf):
        following = up(f + 1) if f + 1 < nf else None
        acc = _dot(act(f, *pending), wd_ref[f * tf:(f + 1) * tf, :]) + acc
        pending = following
    if final_norm:
        acc = _rms(acc, fg_ref[...])
    o_ref[...] = acc


def _ffn(x2d, mix_parts, wo, g, wa, wb, cw, cb, wd, b, t, tm, tf, prev=None, final_g=None):
    m = x2d.shape[0]
    paged_prev = prev is not None
    tiles_per_seq = max(t // tm, 1)
    rows = lambda n: pl.BlockSpec((tm, n), lambda i: (i, 0))
    in_specs = ([rows(D_MODEL)] + [rows(p.shape[1]) for p in mix_parts]
                + [_resident(wo.shape), _resident((1, D_MODEL)),
                   _resident((D_MODEL, D_FF)), _resident((D_MODEL, D_FF)), _resident((FFN_CONV_W, D_FF)),
                   _resident((1, D_FF)), _resident((D_FF, D_MODEL))])
    args = [x2d, *mix_parts, wo, g, wa, wb, cw, cb, wd]
    scratch = []
    if paged_prev:
        in_specs += [rows(D_FF)] * 2
        args += list(prev)
        nb_spec = rows(D_FF)
        nb_shape = jax.ShapeDtypeStruct((m, D_FF), F32)
    else:
        nb_spec = pl.BlockSpec((1, FFN_CONV_W - 1, D_FF), lambda i: (i, 0, 0))
        nb_shape = jax.ShapeDtypeStruct((m // tm, FFN_CONV_W - 1, D_FF), F32)
        scratch.append(pltpu.VMEM((SUBLANES, D_FF), F32))
    if final_g is not None:
        in_specs.append(_resident((1, D_MODEL)))
        args.append(final_g)
    return pl.pallas_call(
        functools.partial(_ffn_kernel, tm=tm, tf=tf, n_mix=len(mix_parts), tiles_per_seq=tiles_per_seq,
                          paged_prev=paged_prev, final_norm=final_g is not None),
        grid=(m // tm,),
        in_specs=in_specs,
        out_specs=(rows(D_MODEL), nb_spec),
        out_shape=(jax.ShapeDtypeStruct((m, D_MODEL), F32), nb_shape),
        scratch_shapes=scratch,
        compiler_params=_params(1),
        name="ffn",
    )(*args)


def _lru_kernel(*refs, tm, tiles_per_seq, paged_prev):
    it = iter(refs)
    x_ref, g_ref, wg_ref, wx_ref, cw_ref, cb_ref, wri_ref, bri_ref, lam_ref = (next(it) for _ in range(9))
    if paged_prev:
        p1_ref, p2_ref, p3_ref, h0_ref = (next(it) for _ in range(4))
    mix_ref, u_ref, hl_ref = next(it), next(it), next(it)
    if not paged_prev:
        cu_s, ch_s = next(it), next(it)
    mi = pl.program_id(0)

    h = _rms(x_ref[...], g_ref[...]).astype(BF16)
    u = _dot(h, wx_ref[...])
    gate_in = _dot(h, wg_ref[...])
    u1, u2, u3 = (pltpu.roll(u, k, 0) for k in (1, 2, 3))
    if paged_prev:
        t = lax.broadcasted_iota(jnp.int32, (tm, D_RNN), 0) & (SUBLANES - 1)
        u1 = jnp.where(t == 0, p1_ref[...], u1)
        u2 = jnp.where(t < 2, p2_ref[...], u2)
        u3 = jnp.where(t < 3, p3_ref[...], u3)
        u_ref[...] = u
    else:
        @pl.when((mi % tiles_per_seq) == 0)
        def _():
            cu_s[...] = jnp.zeros(cu_s.shape, F32)
            ch_s[...] = jnp.zeros(ch_s.shape, F32)

        c = cu_s[...]
        row = lax.broadcasted_iota(jnp.int32, (SUBLANES, D_RNN), 0)
        l1, l2, l3 = (c[SUBLANES - k:SUBLANES - k + 1] for k in (1, 2, 3))
        top1 = jnp.where(row == 0, l1, u1[:SUBLANES])
        top2 = jnp.where(row == 0, l2, jnp.where(row == 1, l1, u2[:SUBLANES]))
        top3 = jnp.where(row == 0, l3, jnp.where(row == 1, l2, jnp.where(row == 2, l1, u3[:SUBLANES])))
        u1, u2, u3 = (jnp.concatenate([top, full[SUBLANES:]], axis=0)
                      for top, full in ((top1, u1), (top2, u2), (top3, u3)))
        cu_s[...] = u[tm - SUBLANES:, :]
        u_ref[0] = u[tm - (LRU_CONV_W - 1):, :]
    cw = cw_ref[...]
    xc = cb_ref[...] + u3 * cw[0:1] + u2 * cw[1:2] + u1 * cw[2:3] + u * cw[3:4]

    xb = xc.astype(BF16)
    tile = 2 * LANES
    nt = D_RNN // tile
    near = lambda ct: [kc for kc in (ct - 1, ct, ct + 1) if 0 <= kc < nt]

    def gate_proj(off):
        tiles = []
        for ct in range(nt):
            cols = slice(off + ct * tile, off + (ct + 1) * tile)
            acc = None
            for kc in near(ct):
                part = _dot(xb[:, kc * tile:(kc + 1) * tile], wri_ref[kc * tile:(kc + 1) * tile, cols])
                acc = part if acc is None else part + acc
            tiles.append(bri_ref[:, cols] + acc)
        return jnp.concatenate(tiles, axis=1)

    rg = _sigmoid(gate_proj(0))
    ig = _sigmoid(gate_proj(D_RNN))
    log_a = rg * (LRU_C * _log_sigmoid(lam_ref[...]))
    a_all = jnp.exp(log_a)
    bx_all = jnp.sqrt(-jnp.tanh(log_a) * (a_all * a_all + 1.0)) * (ig * xc)

    r8 = lax.broadcasted_iota(jnp.int32, (SUBLANES, D_RNN), 0)
    hc = None if paged_prev else ch_s[SUBLANES - 1:SUBLANES, :]
    groups = []
    for gi in range(tm // SUBLANES):
        av = a_all[gi * SUBLANES:(gi + 1) * SUBLANES]
        bv = bx_all[gi * SUBLANES:(gi + 1) * SUBLANES]
        for s in (1, 2, 4):
            keep = r8 >= s
            bv = jnp.where(keep, av * pltpu.roll(bv, s, 0) + bv, bv)
            av = jnp.where(keep, av * pltpu.roll(av, s, 0), av)
        if paged_prev:
            hc = h0_ref[gi:gi + 1, :]
        hg = av * hc + bv
        groups.append(hg)
        hc = hg[SUBLANES - 1:SUBLANES, :]
    hs = jnp.concatenate(groups, axis=0)
    if paged_prev:
        hl_ref[...] = hs
    else:
        ch_s[...] = groups[-1]
        hl_ref[0] = hs[tm - 1:, :]
    mix_ref[...] = (hs * _gelu(gate_in)).astype(BF16)


def _lru(x2d, g, wg, wx, cw, cb, wri, bri, lam, b, t, tm, prev=None):
    m = x2d.shape[0]
    paged_prev = prev is not None
    tiles_per_seq = max(t // tm, 1)
    rows = lambda n: pl.BlockSpec((tm, n), lambda i: (i, 0))
    in_specs = [rows(D_MODEL), _resident((1, D_MODEL)), _resident((D_MODEL, D_RNN)), _resident((D_MODEL, D_RNN)),
                _resident((LRU_CONV_W, D_RNN)), _resident((1, D_RNN)), _resident((D_RNN, 2 * D_RNN)),
                _resident((1, 2 * D_RNN)), _resident((1, D_RNN))]
    args = [x2d, g, wg, wx, cw, cb, wri, bri, lam]
    scratch = []
    if paged_prev:
        in_specs += [rows(D_RNN)] * 3 + [_resident(prev[3].shape)]
        args += list(prev)
        u_spec, hl_spec = rows(D_RNN), rows(D_RNN)
        u_shape = hl_shape = jax.ShapeDtypeStruct((m, D_RNN), F32)
    else:
        u_spec = pl.BlockSpec((1, LRU_CONV_W - 1, D_RNN), lambda i: (i, 0, 0))
        hl_spec = pl.BlockSpec((1, 1, D_RNN), lambda i: (i, 0, 0))
        u_shape = jax.ShapeDtypeStruct((m // tm, LRU_CONV_W - 1, D_RNN), F32)
        hl_shape = jax.ShapeDtypeStruct((m // tm, 1, D_RNN), F32)
        scratch += [pltpu.VMEM((SUBLANES, D_RNN), F32)] * 2
    return pl.pallas_call(
        functools.partial(_lru_kernel, tm=tm, tiles_per_seq=tiles_per_seq, paged_prev=paged_prev),
        grid=(m // tm,),
        in_specs=in_specs,
        out_specs=(rows(D_RNN), u_spec, hl_spec),
        out_shape=(jax.ShapeDtypeStruct((m, D_RNN), BF16), u_shape, hl_shape),
        scratch_shapes=scratch,
        compiler_params=_params(1),
        name="lru",
    )(*args)


def _rope_tables(pos):
    half = ROT_DIMS // 2
    inv = ROPE_THETA ** (-jnp.arange(half, dtype=F32) / half)
    ang = pos.astype(F32)[:, None] * inv[None, :]
    lane = jnp.arange(LANES) % DH_DIFF
    cos = jnp.cos(ang)[:, lane % half]
    sin = jnp.sin(ang)[:, lane % half]
    cos_t = jnp.where(lane[None, :] < ROT_DIMS, cos, 1.0)
    sin_t = jnp.where(lane[None, :] < half, -sin, jnp.where(lane[None, :] < ROT_DIMS, sin, 0.0))
    return cos_t, sin_t


def _block_diag(w):
    n, c, d = w.shape
    same = jnp.arange(n)[:, None] == jnp.arange(n)[None, :]
    return jnp.where(same[:, None, :, None], w[:, :, None, :], 0).reshape(n * c, n * d)


def _expand_prev(buf, t, k):
    bs, wm1, c = buf.shape
    n = min(k, t)
    return jnp.pad(buf[:, wm1 - k:wm1 - k + n], ((0, 0), (0, t - n), (0, 0))).reshape(bs * t, c)


def kernel(x_prompt, x_sample, cache_fox_k, cache_fox_v, cache_fox_logf, cache_diff_k, cache_diff_v, state_lru_conv, state_lru_h, state_ffn_conv, page_table, mix_norm_g, ab_w_in, ab_b_f, ab_lam_q1, ab_lam_k1, ab_lam_q2, ab_lam_k2, ab_subln_g, ab_w_out, lru_w_gate, lru_w_x, lru_conv_w, lru_conv_b, lru_w_a, lru_b_a, lru_w_i, lru_b_i, lru_lambda, lru_w_out, ffn_norm_g, ffn_w_a, ffn_w_b, ffn_conv_w, ffn_conv_b, ffn_w_down, final_norm_g):
    depth = mix_norm_g.shape[0]
    assert depth == 2 and ab_w_in.shape[0] == 1 and lru_w_x.shape[0] == 1

    w_in = ab_w_in[0]
    o_f = 3 * D_FOX
    w_all = jnp.concatenate([w_in[:, :o_f], w_in[:, o_f + H_FOX:], w_in[:, o_f:o_f + H_FOX],
                             jnp.zeros((D_MODEL, LANES - H_FOX), F32)], axis=1).astype(BF16)
    bf_pad = jnp.pad(ab_b_f[0], (0, LANES - H_FOX)).reshape(1, LANES)
    lamv = jnp.stack([ab_lam_q1[0], ab_lam_k1[0], ab_lam_q2[0], ab_lam_k2[0]]).astype(F32)
    subln_g = ab_subln_g[0].reshape(1, LANES)
    w_out = ab_w_out[0].astype(BF16)
    wg = lru_w_gate[0].astype(BF16)
    wx = lru_w_x[0].astype(BF16)
    wri = jnp.concatenate([_block_diag(lru_w_a[0]), _block_diag(lru_w_i[0])], axis=1).astype(BF16)
    bri = jnp.concatenate([lru_b_a[0], lru_b_i[0]]).reshape(1, 2 * D_RNN)
    lru_wo = lru_w_out[0].astype(BF16)
    ffn_wa = ffn_w_a.astype(BF16)
    ffn_wb = ffn_w_b.astype(BF16)
    ffn_wd = ffn_w_down.astype(BF16)
    lam_init = 0.8 - 0.6 * math.exp(-0.3 * 0)

    def layer0_proj(x2d, pos_rows, tm, seq_len=None):
        cos_t, sin_t = _rope_tables(pos_rows)
        return _inproj(x2d, mix_norm_g[0].reshape(1, D_MODEL), w_all, bf_pad, cos_t, sin_t, tm, seq_len)

    def tail(x2d, mix0, b, t, tm_ffn, tm_lru, tf, ffn_prev, lru_prev):
        x1, nb0 = _ffn(x2d, mix0, w_out, ffn_norm_g[0].reshape(1, D_MODEL), ffn_wa[0], ffn_wb[0],
                       ffn_conv_w[0], ffn_conv_b[0].reshape(1, D_FF), ffn_wd[0], b, t, tm_ffn, tf,
                       prev=ffn_prev[0])
        mix1, u_out, h_out = _lru(x1, mix_norm_g[1].reshape(1, D_MODEL), wg, wx, lru_conv_w[0],
                                  lru_conv_b[0].reshape(1, D_RNN), wri, bri, lru_lambda[0].reshape(1, D_RNN),
                                  b, t, tm_lru, prev=lru_prev)
        y, nb1 = _ffn(x1, (mix1,), lru_wo, ffn_norm_g[1].reshape(1, D_MODEL), ffn_wa[1], ffn_wb[1],
                      ffn_conv_w[1], ffn_conv_b[1].reshape(1, D_FF), ffn_wd[1], b, t, tm_ffn, tf,
                      prev=ffn_prev[1], final_g=final_norm_g.reshape(1, D_MODEL))
        return y, nb0, nb1, u_out, h_out

    bp, tp, _ = x_prompt.shape
    mp = bp * tp
    xp = x_prompt.reshape(mp, D_MODEL)
    (qf, kf, vf, kfb, vfb, logft, qd, kd, vd, kdb, vdb) = layer0_proj(xp, jnp.arange(tp), 512, tp)
    logf_state = lambda a, b, t: jnp.transpose(a.reshape(H_FOX, b, t), (1, 2, 0))[None]
    fox_state = lambda a: jnp.transpose(a.reshape(bp, H_FOX, DH_FOX, tp), (0, 3, 1, 2))[None]
    cum = _cumsum_prompt(logft, bp, tp)
    r3 = lambda a: a.reshape(bp, tp, D_FOX)
    tq = 512
    of = _attn_prompt(r3(qf), r3(kfb), r3(vfb), (cum,), True, tq, lam_init)
    od = _attn_prompt(r3(qd), r3(kdb), r3(vdb), (lamv, subln_g), False, tq, lam_init)
    mix0 = (of.reshape(mp, D_FOX), od.reshape(mp, D_DIFF))
    tm_ffn, tm_lru = 256, 256
    y_p, nb0, nb1, u_out, h_out = tail(xp, mix0, bp, tp, tm_ffn, tm_lru, 512, (None, None), None)
    seq_last = lambda a, tm: a[tp // tm - 1::tp // tm]
    nb0, nb1, u_out, h_out = seq_last(nb0, tm_ffn), seq_last(nb1, tm_ffn), seq_last(u_out, tm_lru), seq_last(h_out, tm_lru)
    y_prompt = y_p.reshape(bp, tp, D_MODEL)
    p_state = (fox_state(kf), fox_state(vf),
               logf_state(logft, bp, tp),
               kd.reshape(1, bp, tp, H_DIFF, 2 * DH_DIFF), vd.reshape(1, bp, tp, H_DIFF, 2 * DH_DIFF),
               u_out[None], h_out.reshape(1, bp, D_RNN), jnp.stack([nb0, nb1]))

    bs, ts, _ = x_sample.shape
    assert ts == SUBLANES
    ms = bs * ts
    n_pages = page_table.shape[1]
    past = n_pages * PAGE_SIZE
    xs = x_sample.reshape(ms, D_MODEL)
    pos_rows = jnp.tile(past + jnp.arange(ts), bs)
    (qf, kf, vf, kfb, vfb, logft, qd, kd, vd, kdb, vdb) = layer0_proj(xs, pos_rows, ms)
    n_pool = cache_fox_k.shape[1]
    logf_pool = jnp.swapaxes(cache_fox_logf[0], 1, 2).reshape(n_pool * H_FOX, PAGE_SIZE)
    lc = _cumsum_rows(logf_pool).reshape(n_pool, H_FOX, PAGE_SIZE)
    logfn = jnp.pad(logft.reshape(H_FOX, bs, ts).transpose(1, 0, 2), ((0, 0), (0, 0), (0, LANES - ts)))
    pad_rows = lambda a: jnp.pad(a.reshape(bs, ts, D_FOX), ((0, 0), (0, PAGE_SIZE - ts), (0, 0)))
    kt_pages = lambda c: jnp.transpose(c, (0, 2, 3, 1)).reshape(n_pool, D_FOX, PAGE_SIZE)
    mix0 = _attn_decode(page_table, qf.astype(F32).reshape(bs, ts, D_FOX), qd.astype(F32).reshape(bs, ts, D_DIFF),
                        pad_rows(kfb), pad_rows(vfb), pad_rows(kdb), pad_rows(vdb),
                        kt_pages(cache_fox_k[0]), kt_pages(cache_fox_v[0]), lc,
                        cache_diff_k[0].reshape(n_pool, PAGE_SIZE * H_DIFF, 2 * DH_DIFF),
                        cache_diff_v[0].reshape(n_pool, PAGE_SIZE * H_DIFF, 2 * DH_DIFF),
                        logfn, lamv, subln_g, 16, lam_init).reshape(ms, D_MIX_AB)
    ffn_prev = tuple((_expand_prev(state_ffn_conv[l], ts, 1), _expand_prev(state_ffn_conv[l], ts, 2))
                     for l in range(depth))
    lru_prev = tuple(_expand_prev(state_lru_conv[0], ts, k) for k in (1, 2, 3)) + (state_lru_h[0],)
    y_s, a0, a1, u_full, h_full = tail(xs, (mix0,), bs, ts, ms, ms, 512, ffn_prev, lru_prev)
    y_sample = y_s.reshape(bs, ts, D_MODEL)
    last = lambda a, k: a.reshape(bs, ts, -1)[:, ts - k:]
    s_state = (kf.reshape(1, bs, ts, H_FOX, DH_FOX), vf.reshape(1, bs, ts, H_FOX, DH_FOX),
               logf_state(logft, bs, ts),
               kd.reshape(1, bs, ts, H_DIFF, 2 * DH_DIFF), vd.reshape(1, bs, ts, H_DIFF, 2 * DH_DIFF),
               last(u_full, LRU_CONV_W - 1)[None], last(h_full, 1).reshape(1, bs, D_RNN),
               jnp.stack([last(a0, FFN_CONV_W - 1), last(a1, FFN_CONV_W - 1)]))

    return (y_prompt, y_sample) + p_state + s_state
```
